```python
import math
import jax, jax.numpy as jnp
from jax import lax
import numpy as np

D_MODEL = 1024
BATCH = 4
SEQ = 8192
DEPTH = 2

CHUNK = 64
SB_BLOCK = 128
RET_HEADS = 4
RET_DIM = 128
SB_HEADS = 4
SB_DIM = 128
SSM_HEADS = 8
SSM_HEAD_DIM = 64
SSM_STATE = 128
SSM_GROUPS = 2
SSM_CONV = 4
D_FF = 2816
ROPE_BASE = 10000.0
NORM_EPS = 1e-6
N_SUB = 3

RET_W = RET_HEADS * RET_DIM
SB_W = SB_HEADS * SB_DIM
SSM_W = SSM_HEADS * SSM_HEAD_DIM
MIX_W = RET_W + SB_W + SSM_W
SSM_XBC = SSM_W + 2 * SSM_GROUPS * SSM_STATE
IN_W = 4 * RET_W + 3 * SB_W + SSM_W + SSM_XBC + SSM_HEADS

kernel_name = "hybrid_ret_sb_ssd_macaron_adaln"


def rmsnorm(x, gain):
    xf = x.astype(jnp.float32)
    y = xf * lax.rsqrt(jnp.mean(xf * xf, axis=-1, keepdims=True) + NORM_EPS)
    return (y * gain.astype(jnp.float32)).astype(x.dtype)


def modulate(h, shift, scale):
    return h * (1.0 + scale) + shift


def swiglu(u, wg, wu, wd):
    return (jax.nn.silu(u @ wg) * (u @ wu)) @ wd


def rope(x, pos):
    half = x.shape[-1] // 2
    inv_freq = ROPE_BASE ** (-jnp.arange(half, dtype=jnp.float32) / half)
    ang = pos[:, None] * inv_freq[None, :]
    cos = jnp.cos(ang)[None, :, None, :]
    sin = jnp.sin(ang)[None, :, None, :]
    x1 = x[..., :half].astype(jnp.float32)
    x2 = x[..., half:].astype(jnp.float32)
    return jnp.concatenate([x1 * cos - x2 * sin, x1 * sin + x2 * cos], axis=-1)


def retention(q, k, v, g, gn_gain):
    B, S, _ = q.shape
    nc = S // CHUNK
    H, Dh = RET_HEADS, RET_DIM
    pos = jnp.arange(S, dtype=jnp.float32)
    q = rope(q.reshape(B, S, H, Dh), pos)
    k = rope(k.reshape(B, S, H, Dh), pos) * (Dh ** -0.5)
    v = v.reshape(B, S, H, Dh).astype(jnp.float32)
    log_gamma = jnp.log1p(-(2.0 ** (-5.0 - jnp.arange(H, dtype=jnp.float32))))
    idx = jnp.arange(CHUNK, dtype=jnp.float32)
    dmat = jnp.exp(log_gamma[:, None, None] * jnp.abs(idx[:, None] - idx[None, :]))
    qc = q.reshape(B, nc, CHUNK, H, Dh)
    kc = k.reshape(B, nc, CHUNK, H, Dh)
    vc = v.reshape(B, nc, CHUNK, H, Dh)
    scores = jnp.einsum('bclhd,bcshd->bchls', qc, kc) * dmat[None, None]
    y_intra = jnp.einsum('bchls,bcshe->bclhe', scores, vc)
    k_decay = jnp.exp(log_gamma[:, None] * (CHUNK - 1 - idx)[None, :])
    kv = jnp.einsum('bcshd,hs,bcshe->bchde', kc, k_decay, vc).astype(jnp.float32)
    chunk_decay = jnp.exp(log_gamma * CHUNK)[None, :, None, None]

    def step(state, kv_c):
        return state * chunk_decay + kv_c, state

    _, s_prev = lax.scan(step, jnp.zeros((B, H, Dh, Dh), jnp.float32), jnp.moveaxis(kv, 1, 0))
    s_prev = jnp.moveaxis(s_prev, 0, 1)
    q_decay = jnp.exp(log_gamma[:, None] * (idx + 1.0)[None, :])
    y_cross = jnp.einsum('bclhd,hl,bchde->bclhe', qc, q_decay, s_prev)
    y = (y_intra + y_cross).reshape(B, S, H, Dh)
    y = rmsnorm(y, gn_gain.reshape(H, Dh)).reshape(B, S, RET_W)
    return y * jax.nn.silu(g.astype(jnp.float32))


def stick_breaking(q, k, v):
    B, S, _ = q.shape
    H, Dh = SB_HEADS, SB_DIM
    q = q.reshape(B, S, H, Dh).transpose(0, 2, 1, 3)
    k = k.reshape(B, S, H, Dh).transpose(0, 2, 1, 3)
    v = v.reshape(B, S, H, Dh).transpose(0, 2, 1, 3)
    scale = Dh ** -0.5
    outs = []
    for i in range(S // SB_BLOCK):
        q0 = i * SB_BLOCK
        kend = q0 + SB_BLOCK
        qb = q[:, :, q0:kend]
        kb = k[:, :, :kend]
        vb = v[:, :, :kend]
        z = jnp.einsum('bhtd,bhsd->bhts', qb, kb).astype(jnp.float32) * scale
        t_pos = q0 + jnp.arange(SB_BLOCK)
        s_pos = jnp.arange(kend)
        visible = s_pos[None, :] < t_pos[:, None]
        log_beta = jax.nn.log_sigmoid(z)
        log_keep = jnp.where(visible, jax.nn.log_sigmoid(-z), 0.0)
        tail = lax.cumsum(log_keep, axis=3, reverse=True) - log_keep
        w = jnp.where(visible, jnp.exp(log_beta + tail), 0.0)
        outs.append(jnp.einsum('bhts,bhsd->bhtd', w, vb.astype(jnp.float32)))
    y = jnp.concatenate(outs, axis=2)
    return y.transpose(0, 2, 1, 3).reshape(B, S, SB_W)


def mamba2(z, xbc, dt_raw, conv_w, conv_b, dt_bias, a_log, d_skip, norm_gain):
    B, S, _ = xbc.shape
    G, Hg, P, N = SSM_GROUPS, SSM_HEADS // SSM_GROUPS, SSM_HEAD_DIM, SSM_STATE
    nc = S // CHUNK
    xbc = lax.conv_general_dilated(
        xbc, conv_w[:, None, :], window_strides=(1,), padding=[(SSM_CONV - 1, 0)],
        dimension_numbers=('NWC', 'WIO', 'NWC'), feature_group_count=SSM_XBC) + conv_b
    xbc = jax.nn.silu(xbc).astype(jnp.float32)
    xs = xbc[..., :SSM_W].reshape(B, nc, CHUNK, G, Hg, P)
    bm = xbc[..., SSM_W:SSM_W + G * N].reshape(B, nc, CHUNK, G, N)
    cm = xbc[..., SSM_W + G * N:].reshape(B, nc, CHUNK, G, N)
    dt = jax.nn.softplus(dt_raw.astype(jnp.float32) + dt_bias).reshape(B, nc, CHUNK, G, Hg)
    a = -jnp.exp(a_log.astype(jnp.float32)).reshape(G, Hg)
    acum = jnp.cumsum(dt * a, axis=2)
    xdt = xs * dt[..., None]
    causal = jnp.tril(jnp.ones((CHUNK, CHUNK), dtype=bool))[None, None, :, :, None, None]
    seg = acum[:, :, :, None] - acum[:, :, None, :]
    decay = jnp.exp(jnp.where(causal, seg, -jnp.inf))
    cb = jnp.einsum('bclgn,bcsgn->bcgls', cm, bm)
    y_intra = jnp.einsum('bcgls,bclsgh,bcsghp->bclghp', cb, decay, xdt)
    decay_end = jnp.exp(acum[:, :, -1:] - acum)
    states = jnp.einsum('bcsgn,bcsgh,bcsghp->bcghpn', bm, decay_end, xdt)
    chunk_decay = jnp.exp(acum[:, :, -1])

    def step(h, inp):
        st, dec = inp
        return h * dec[..., None, None] + st, h

    _, h_prev = lax.scan(step, jnp.zeros((B, G, Hg, P, N), jnp.float32),
                         (jnp.moveaxis(states, 1, 0), jnp.moveaxis(chunk_decay, 1, 0)))
    h_prev = jnp.moveaxis(h_prev, 0, 1)
    y_inter = jnp.einsum('bclgn,bcghpn,bclgh->bclghp', cm, h_prev, jnp.exp(acum))
    y = y_intra + y_inter + xs * d_skip.astype(jnp.float32).reshape(G, Hg)[..., None]
    y = y.reshape(B, S, SSM_W)
    return rmsnorm(y * jax.nn.silu(z.astype(jnp.float32)), norm_gain)


def setup_inputs(seed: int = 0) -> dict:
    key = jax.random.key(seed)
    ks = jax.random.split(key, 32)
    f32 = jnp.float32
    nrm = lambda k, shape, s: jax.random.normal(k, shape, f32) * s
    gain = lambda k, shape: 1.0 + 0.01 * jax.random.normal(k, shape, f32)
    dt0 = jnp.exp(jax.random.uniform(ks[11], (DEPTH, SSM_HEADS), f32, math.log(1e-3), math.log(1e-1)))
    return {
        "x": nrm(ks[0], (BATCH, SEQ, D_MODEL), 1.0),
        "c": nrm(ks[1], (BATCH, D_MODEL), 1.0),
        "ada_w": nrm(ks[2], (DEPTH, D_MODEL, 3 * N_SUB * D_MODEL), 0.1 * D_MODEL ** -0.5),
        "ada_b": nrm(ks[3], (DEPTH, 3 * N_SUB * D_MODEL), 0.01),
        "norm_ffn1": gain(ks[4], (DEPTH, D_MODEL)),
        "ffn1_wg": nrm(ks[5], (DEPTH, D_MODEL, D_FF), D_MODEL ** -0.5),
        "ffn1_wu": nrm(ks[6], (DEPTH, D_MODEL, D_FF), D_MODEL ** -0.5),
        "ffn1_wd": nrm(ks[7], (DEPTH, D_FF, D_MODEL), D_FF ** -0.5),
        "norm_mix": gain(ks[8], (DEPTH, D_MODEL)),
        "w_in": nrm(ks[9], (DEPTH, D_MODEL, IN_W), D_MODEL ** -0.5),
        "conv_w": nrm(ks[10], (DEPTH, SSM_CONV, SSM_XBC), SSM_CONV ** -0.5),
        "conv_b": nrm(ks[12], (DEPTH, SSM_XBC), 0.01),
        "dt_bias": dt0 + jnp.log(-jnp.expm1(-dt0)),
        "a_log": jnp.log(jax.random.uniform(ks[13], (DEPTH, SSM_HEADS), f32, 1.0, 16.0)),
        "d_skip": gain(ks[14], (DEPTH, SSM_HEADS)),
        "ret_gn": gain(ks[15], (DEPTH, RET_W)),
        "ssm_norm": gain(ks[16], (DEPTH, SSM_W)),
        "w_out": nrm(ks[17], (DEPTH, MIX_W, D_MODEL), MIX_W ** -0.5),
        "norm_ffn2": gain(ks[18], (DEPTH, D_MODEL)),
        "ffn2_wg": nrm(ks[19], (DEPTH, D_MODEL, D_FF), D_MODEL ** -0.5),
        "ffn2_wu": nrm(ks[20], (DEPTH, D_MODEL, D_FF), D_MODEL ** -0.5),
        "ffn2_wd": nrm(ks[21], (DEPTH, D_FF, D_MODEL), D_FF ** -0.5),
        "final_ada_w": nrm(ks[22], (D_MODEL, 2 * D_MODEL), 0.1 * D_MODEL ** -0.5),
        "final_ada_b": nrm(ks[23], (2 * D_MODEL,), 0.01),
        "final_norm": gain(ks[24], (D_MODEL,)),
    }


def reference(x, c, ada_w, ada_b, norm_ffn1, ffn1_wg, ffn1_wu, ffn1_wd, norm_mix, w_in,
              conv_w, conv_b, dt_bias, a_log, d_skip, ret_gn, ssm_norm, w_out,
              norm_ffn2, ffn2_wg, ffn2_wu, ffn2_wd, final_ada_w, final_ada_b, final_norm):
    B, S, _ = x.shape
    cond = jax.nn.silu(c)
    splits = [RET_W, 2 * RET_W, 3 * RET_W, 4 * RET_W,
              4 * RET_W + SB_W, 4 * RET_W + 2 * SB_W, 4 * RET_W + 3 * SB_W,
              4 * RET_W + 3 * SB_W + SSM_W, 4 * RET_W + 3 * SB_W + SSM_W + SSM_XBC]
    h = x
    for l in range(DEPTH):
        mod = (cond @ ada_w[l] + ada_b[l]).reshape(B, 3 * N_SUB, D_MODEL)[:, None]
        u = modulate(rmsnorm(h, norm_ffn1[l]), mod[:, :, 0], mod[:, :, 1])
        h = h + 0.5 * (1.0 + mod[:, :, 2]) * swiglu(u, ffn1_wg[l], ffn1_wu[l], ffn1_wd[l])
        u = modulate(rmsnorm(h, norm_mix[l]), mod[:, :, 3], mod[:, :, 4])
        proj = u @ w_in[l]
        rq, rk, rv, rg, sq, sk, sv, mz, mxbc, mdt = jnp.split(proj, splits, axis=-1)
        y_ret = retention(rq, rk, rv, rg, ret_gn[l]).astype(x.dtype)
        y_sb = stick_breaking(sq, sk, sv).astype(x.dtype)
        y_ssm = mamba2(mz, mxbc, mdt, conv_w[l], conv_b[l], dt_bias[l], a_log[l],
                       d_skip[l], ssm_norm[l]).astype(x.dtype)
        mixed = jnp.concatenate([y_ret, y_sb, y_ssm], axis=-1) @ w_out[l]
        h = h + (1.0 + mod[:, :, 5]) * mixed
        u = modulate(rmsnorm(h, norm_ffn2[l]), mod[:, :, 6], mod[:, :, 7])
        h = h + 0.5 * (1.0 + mod[:, :, 8]) * swiglu(u, ffn2_wg[l], ffn2_wu[l], ffn2_wd[l])
    fmod = (cond @ final_ada_w + final_ada_b).reshape(B, 2, D_MODEL)[:, None]
    return modulate(rmsnorm(h, final_norm), fmod[:, :, 0], fmod[:, :, 1])
```

```python
import functools
import math

import jax
import jax.numpy as jnp
from jax import lax
from jax.experimental import pallas as pl
from jax.experimental.pallas import tpu as pltpu

D_MODEL = 1024
DEPTH = 2
CHUNK = 64
RET_HEADS = 4
RET_DIM = 128
SB_HEADS = 4
SB_DIM = 128
SSM_HEADS = 8
SSM_HEAD_DIM = 64
SSM_STATE = 128
SSM_GROUPS = 2
SSM_CONV = 4
D_FF = 2816
ROPE_BASE = 10000.0
NORM_EPS = 1e-6
N_SUB = 3

RET_W = RET_HEADS * RET_DIM
SB_W = SB_HEADS * SB_DIM
SSM_W = SSM_HEADS * SSM_HEAD_DIM
MIX_W = RET_W + SB_W + SSM_W
SSM_XBC = SSM_W + 2 * SSM_GROUPS * SSM_STATE
PROJ_W = 4 * RET_W + 3 * SB_W + SSM_W + SSM_XBC
LANES = 128
DT_PAD = LANES

F32 = jnp.float32
BF16 = jnp.bfloat16

TOKEN_TILE = 512
FF_CHUNK = 256
RET_TILE = 256
SB_TILE = 256
SSM_TILE = 256
SB_DEAD_LOG_WEIGHT = -104.0
VMEM_LIMIT = 56 * 1024 * 1024


def _cparams(*sem):
    return pltpu.CompilerParams(dimension_semantics=sem, vmem_limit_bytes=VMEM_LIMIT)


def _silu(x):
    return x / (1.0 + jnp.exp(-x))


def _norm_mod(h, gain, shift, scale):
    y = h * lax.rsqrt(jnp.mean(h * h, axis=-1, keepdims=True) + NORM_EPS) * gain
    return y * (1.0 + scale) + shift


def _ada_kernel(c_ref, w_ref, b_ref, o_ref):
    c = c_ref[...]
    cond = _silu(c)
    o_ref[...] = jnp.dot(cond, w_ref[...], precision=lax.Precision.HIGHEST,
                         preferred_element_type=F32) + b_ref[...]


def _ada_proj(c_pad, w, b, tn):
    L, D, N = w.shape
    return pl.pallas_call(
        _ada_kernel,
        grid=(L, N // tn),
        in_specs=[pl.BlockSpec((8, D), lambda l, j: (0, 0)),
                  pl.BlockSpec((None, D, tn), lambda l, j: (l, 0, j)),
                  pl.BlockSpec((None, 1, tn), lambda l, j: (l, 0, j))],
        out_specs=pl.BlockSpec((None, 8, tn), lambda l, j: (l, 0, j)),
        out_shape=jax.ShapeDtypeStruct((L, 8, N), F32),
        compiler_params=_cparams("arbitrary", "arbitrary"),
        name="ada_proj",
    )(c_pad, w, b)


def _ffn_kernel(h_ref, mod_ref, gain_ref, wg_ref, wu_ref, wd_ref, o_ref):
    h = h_ref[...]
    u = _norm_mod(h, gain_ref[...], mod_ref[0:1, :], mod_ref[1:2, :]).astype(BF16)
    acc = jnp.zeros(h.shape, F32)
    for j in range(D_FF // FF_CHUNK):
        cols = slice(j * FF_CHUNK, (j + 1) * FF_CHUNK)
        g = jnp.dot(u, wg_ref[:, cols], preferred_element_type=F32)
        up = jnp.dot(u, wu_ref[:, cols], preferred_element_type=F32)
        a = (_silu(g) * up).astype(BF16)
        acc = acc + jnp.dot(a, wd_ref[cols, :], preferred_element_type=F32)
    o_ref[...] = h + (0.5 * (1.0 + mod_ref[2:3, :])) * acc


def _ffn(h, mod3, gain, wg, wu, wd):
    B, S, D = h.shape
    tm = min(TOKEN_TILE, S)
    const = lambda b, i: (0, 0)
    return pl.pallas_call(
        _ffn_kernel,
        grid=(B, S // tm),
        in_specs=[pl.BlockSpec((None, tm, D), lambda b, i: (b, i, 0)),
                  pl.BlockSpec((None, 3, D), lambda b, i: (b, 0, 0)),
                  pl.BlockSpec((1, D), const),
                  pl.BlockSpec((D, D_FF), const),
                  pl.BlockSpec((D, D_FF), const),
                  pl.BlockSpec((D_FF, D), const)],
        out_specs=pl.BlockSpec((None, tm, D), lambda b, i: (b, i, 0)),
        out_shape=jax.ShapeDtypeStruct((B, S, D), F32),
        compiler_params=_cparams("arbitrary", "arbitrary"),
        name="ffn",
    )(h, mod3, gain, wg, wu, wd)


def _inproj_kernel(h_ref, mod_ref, gain_ref, w_ref, proj_ref, dt_ref):
    u = _norm_mod(h_ref[...], gain_ref[...], mod_ref[0:1, :], mod_ref[1:2, :]).astype(BF16)
    step = 512
    for j in range(PROJ_W // step):
        cols = slice(j * step, (j + 1) * step)
        proj_ref[:, cols] = jnp.dot(u, w_ref[:, cols], preferred_element_type=F32).astype(proj_ref.dtype)
    dt_ref[...] = jnp.dot(u, w_ref[:, PROJ_W:PROJ_W + DT_PAD], preferred_element_type=F32)


def _inproj(h, mod3, gain, w_pad):
    B, S, D = h.shape
    tm = min(TOKEN_TILE, S)
    const = lambda b, i: (0, 0)
    return pl.pallas_call(
        _inproj_kernel,
        grid=(B, S // tm),
        in_specs=[pl.BlockSpec((None, tm, D), lambda b, i: (b, i, 0)),
                  pl.BlockSpec((None, 3, D), lambda b, i: (b, 0, 0)),
                  pl.BlockSpec((1, D), const),
                  pl.BlockSpec((D, PROJ_W + DT_PAD), const)],
        out_specs=[pl.BlockSpec((None, tm, PROJ_W), lambda b, i: (b, i, 0)),
                   pl.BlockSpec((None, tm, DT_PAD), lambda b, i: (b, i, 0))],
        out_shape=[jax.ShapeDtypeStruct((B, S, PROJ_W), BF16),
                   jax.ShapeDtypeStruct((B, S, DT_PAD), F32)],
        compiler_params=_cparams("arbitrary", "arbitrary"),
        name="inproj",
    )(h, mod3, gain, w_pad)


def _ret_kernel(q_ref, k_ref, v_ref, g_ref, cos_ref, sin_ref, dmat_ref, qdec_ref, kdec_ref,
                sdec_ref, gn_ref, o_ref, state_ref):
    @pl.when(pl.program_id(2) == 0)
    def _():
        state_ref[...] = jnp.zeros_like(state_ref)

    cos2 = cos_ref[...]
    sin2 = sin_ref[...]

    def rope(x):
        return x * cos2 + pltpu.roll(x, RET_DIM // 2, 1) * sin2

    qr = rope(q_ref[...].astype(F32)).astype(BF16)
    kr = rope(k_ref[...].astype(F32)) * (RET_DIM ** -0.5)
    v = v_ref[...].astype(F32)
    state = state_ref[...]

    scores = lax.dot_general(qr, kr.astype(BF16), (((1,), (1,)), ((), ())),
                             preferred_element_type=F32) * dmat_ref[...]
    y = jnp.dot(scores.astype(BF16), v_ref[...], preferred_element_type=F32)
    y = y + qdec_ref[...] * jnp.dot(qr, state.astype(BF16), preferred_element_type=F32)

    kv = jnp.dot(kr.T.astype(BF16), (kdec_ref[...] * v).astype(BF16), preferred_element_type=F32)
    state_ref[...] = state * sdec_ref[0:1, :] + kv

    y = y * lax.rsqrt(jnp.mean(y * y, axis=-1, keepdims=True) + NORM_EPS) * gn_ref[...]
    o_ref[...] = (y * _silu(g_ref[...].astype(F32))).astype(o_ref.dtype)


def _retention(proj, gn, consts):
    B, S, _ = proj.shape
    T = min(RET_TILE, S)
    H, Dh = RET_HEADS, RET_DIM
    cos2, sin2, dmat, qdec, kdec, sdec = consts
    col = lambda off: (lambda b, h, c: (b, c, off + h))
    per_head = lambda b, h, c: (h, 0, 0)
    return pl.pallas_call(
        _ret_kernel,
        grid=(B, H, S // T),
        in_specs=[pl.BlockSpec((None, T, Dh), col(0)),
                  pl.BlockSpec((None, T, Dh), col(H)),
                  pl.BlockSpec((None, T, Dh), col(2 * H)),
                  pl.BlockSpec((None, T, Dh), col(3 * H)),
                  pl.BlockSpec((T, Dh), lambda b, h, c: (c, 0)),
                  pl.BlockSpec((T, Dh), lambda b, h, c: (c, 0)),
                  pl.BlockSpec((None, T, T), per_head),
                  pl.BlockSpec((None, T, Dh), per_head),
                  pl.BlockSpec((None, T, Dh), per_head),
                  pl.BlockSpec((None, 8, Dh), per_head),
                  pl.BlockSpec((None, 1, Dh), per_head)],
        out_specs=pl.BlockSpec((None, T, Dh), lambda b, h, c: (b, c, h)),
        out_shape=jax.ShapeDtypeStruct((B, S, RET_W), BF16),
        scratch_shapes=[pltpu.VMEM((Dh, Dh), F32)],
        compiler_params=_cparams("arbitrary", "arbitrary", "arbitrary"),
        name="retention",
    )(proj, proj, proj, proj, cos2, sin2, dmat, qdec, kdec, sdec, gn)


def _retention_consts(S):
    T = min(RET_TILE, S)
    H, Dh = RET_HEADS, RET_DIM
    half = Dh // 2
    pos = jnp.arange(S, dtype=F32)
    inv_freq = ROPE_BASE ** (-jnp.arange(half, dtype=F32) / half)
    ang = pos[:, None] * inv_freq[None, :]
    cos, sin = jnp.cos(ang), jnp.sin(ang)
    cos2 = jnp.concatenate([cos, cos], axis=-1)
    sin2 = jnp.concatenate([-sin, sin], axis=-1)
    log_gamma = jnp.log1p(-(2.0 ** (-5.0 - jnp.arange(H, dtype=F32))))
    idx = jnp.arange(T, dtype=F32)
    chunk_of = jnp.arange(T) // CHUNK
    seen = chunk_of[None, :] <= chunk_of[:, None]
    dmat = jnp.where(seen[None], jnp.exp(log_gamma[:, None, None] * jnp.abs(idx[:, None] - idx[None, :])), 0.0)
    qdec = jnp.exp(log_gamma[:, None] * (idx + 1.0)[None, :])
    kdec = jnp.exp(log_gamma[:, None] * (T - 1 - idx)[None, :])
    sdec = jnp.exp(log_gamma * T)
    bc = lambda a: jnp.broadcast_to(a[:, :, None], (H, T, Dh))
    return cos2, sin2, dmat, bc(qdec), bc(kdec), jnp.broadcast_to(sdec[:, None, None], (H, 8, Dh))


def _split3(x):
    hi = x.astype(BF16)
    r = x - hi.astype(F32)
    mid = r.astype(BF16)
    lo = (r - mid.astype(F32)).astype(BF16)
    return hi, mid, lo


def _sb_kernel(q_ref, k_ref, v_ref, o_ref, acc_ref, carry_ref, *, tq):
    i = pl.program_id(2)
    q = q_ref[...]
    row = lax.broadcasted_iota(jnp.int32, (tq, tq), 0)
    col = lax.broadcasted_iota(jnp.int32, (tq, tq), 1)
    visible = col < row
    later = jnp.where(row > col, 1.0, 0.0).astype(BF16)

    def block(j, diagonal):
        start = pl.multiple_of(j * tq, tq)
        k = k_ref[pl.ds(start, tq), :]
        v = v_ref[pl.ds(start, tq), :]
        z = lax.dot_general(q, k, (((1,), (1,)), ((), ())), preferred_element_type=F32) * (SB_DIM ** -0.5)
        soft = jnp.log1p(jnp.exp(-jnp.abs(z)))
        log_beta = jnp.minimum(z, 0.0) - soft
        log_keep = jnp.minimum(-z, 0.0) - soft
        if diagonal:
            log_keep = jnp.where(visible, log_keep, 0.0)
        hi, mid, lo = _split3(log_keep)
        tail_in = (jnp.dot(hi, later, preferred_element_type=F32)
                   + jnp.dot(mid, later, preferred_element_type=F32)
                   + jnp.dot(lo, later, preferred_element_type=F32))
        carry = carry_ref[...]
        w = jnp.exp(log_beta + tail_in + carry)
        if diagonal:
            w = jnp.where(visible, w, 0.0)
        acc_ref[...] += jnp.dot(w.astype(BF16), v, preferred_element_type=F32)
        new_carry = carry + tail_in[:, 0:1] + log_keep[:, 0:1]
        carry_ref[...] = new_carry
        return jnp.max(new_carry)

    acc_ref[...] = jnp.zeros_like(acc_ref)
    carry_ref[...] = jnp.zeros_like(carry_ref)
    top = block(i, True)

    def cond(s):
        j, top = s
        return jnp.logical_and(j >= 0, top > SB_DEAD_LOG_WEIGHT)

    def body(s):
        j, _ = s
        return j - 1, block(j, False)

    lax.while_loop(cond, body, (i - 1, top))
    o_ref[...] = acc_ref[...].astype(o_ref.dtype)


def _stick_breaking(proj):
    B, S, _ = proj.shape
    T = min(SB_TILE, S)
    H, Dh = SB_HEADS, SB_DIM
    base = 4 * RET_HEADS
    return pl.pallas_call(
        functools.partial(_sb_kernel, tq=T),
        grid=(B, H, S // T),
        in_specs=[pl.BlockSpec((None, T, Dh), lambda b, h, i: (b, i, base + h)),
                  pl.BlockSpec((None, S, Dh), lambda b, h, i: (b, 0, base + H + h)),
                  pl.BlockSpec((None, S, Dh), lambda b, h, i: (b, 0, base + 2 * H + h))],
        out_specs=pl.BlockSpec((None, T, Dh), lambda b, h, i: (b, i, h)),
        out_shape=jax.ShapeDtypeStruct((B, S, SB_W), BF16),
        scratch_shapes=[pltpu.VMEM((T, Dh), F32), pltpu.VMEM((T, 1), F32)],
        compiler_params=_cparams("arbitrary", "arbitrary", "arbitrary"),
        name="stick_breaking",
    )(proj, proj, proj)


def _expand_heads(v):
    T = v.shape[0]
    lane = lax.broadcasted_iota(jnp.int32, (T, LANES), 1)
    tiles = []
    for j in range(SSM_W // LANES):
        a = jnp.broadcast_to(v[:, 2 * j:2 * j + 1], (T, LANES))
        b = jnp.broadcast_to(v[:, 2 * j + 1:2 * j + 2], (T, LANES))
        tiles.append(jnp.where(lane < SSM_HEAD_DIM, a, b))
    return jnp.concatenate(tiles, axis=1)


def _ssm_kernel(xbc_ref, z_ref, dt_ref, convw_ref, convb_ref, dtb_ref, alog_ref, dskip_ref, gain_ref,
                o_ref, xbuf_ref, state_ref, *, T):
    G, Hg, P, N = SSM_GROUPS, SSM_HEADS // SSM_GROUPS, SSM_HEAD_DIM, SSM_STATE
    PAD = 8

    @pl.when(pl.program_id(1) == 0)
    def _():
        xbuf_ref[0:PAD, :] = jnp.zeros((PAD, SSM_XBC), F32)
        state_ref[...] = jnp.zeros_like(state_ref)

    xbuf_ref[PAD:PAD + T, :] = xbc_ref[...].astype(F32)
    conv = convb_ref[...]
    for k in range(SSM_CONV):
        off = PAD - (SSM_CONV - 1) + k
        conv = conv + convw_ref[k:k + 1, :] * xbuf_ref[off:off + T, :]
    xbuf_ref[0:PAD, :] = xbuf_ref[T:T + PAD, :]
    xc = _silu(conv)
    xs = xc[:, :SSM_W]

    dt_in = dt_ref[...] + dtb_ref[...]
    dt = jnp.maximum(dt_in, 0.0) + jnp.log1p(jnp.exp(-jnp.abs(dt_in)))
    da = dt * (-jnp.exp(alog_ref[...]))
    row = lax.broadcasted_iota(jnp.int32, (T, T), 0)
    col = lax.broadcasted_iota(jnp.int32, (T, T), 1)
    causal = row >= col
    acum = jnp.dot(jnp.where(causal, 1.0, 0.0), da, precision=lax.Precision.HIGHEST,
                   preferred_element_type=F32)
    acum_t = acum.T
    last = acum[T - 1:T, :]
    dt_x = _expand_heads(dt)
    grow_x = _expand_heads(jnp.exp(acum))
    end_x = _expand_heads(jnp.exp(last - acum))
    xdt = xs * dt_x
    xdt_b = xdt.astype(BF16)
    xend_b = (xdt * end_x).astype(BF16)

    ys = []
    for g in range(G):
        bm = xc[:, SSM_W + g * N:SSM_W + (g + 1) * N]
        cm = xc[:, SSM_W + G * N + g * N:SSM_W + G * N + (g + 1) * N].astype(BF16)
        cb = lax.dot_general(cm, bm.astype(BF16), (((1,), (1,)), ((), ())), preferred_element_type=F32)
        heads = []
        for hh in range(Hg):
            h = g * Hg + hh
            seg = acum[:, h:h + 1] - acum_t[h:h + 1, :]
            w = (cb * jnp.where(causal, jnp.exp(seg), 0.0)).astype(BF16)
            heads.append(jnp.dot(w, xdt_b[:, h * P:(h + 1) * P], preferred_element_type=F32))
        y_intra = jnp.concatenate(heads, axis=1)
        gc = slice(g * Hg * P, (g + 1) * Hg * P)
        state = state_ref[g]
        y_inter = jnp.dot(cm, state.astype(BF16), preferred_element_type=F32) * grow_x[:, gc]
        ys.append(y_intra + y_inter)
        state_ref[g] = state * grow_x[T - 1:T, gc] + jnp.dot(
            bm.T.astype(BF16), xend_b[:, gc], preferred_element_type=F32)

    y = jnp.concatenate(ys, axis=1) + xs * dskip_ref[...]
    y = y * _silu(z_ref[...].astype(F32))
    o_ref[...] = (y * lax.rsqrt(jnp.mean(y * y, axis=-1, keepdims=True) + NORM_EPS)
                  * gain_ref[...]).astype(o_ref.dtype)


def _ssm(proj, dt_pad, conv_w, conv_b, dt_bias, a_log, d_skip, gain):
    B, S, _ = proj.shape
    T = min(SSM_TILE, S)
    const = lambda b, c: (0, 0)
    pad_heads = lambda a: jnp.pad(a.reshape(1, SSM_HEADS), ((0, 0), (0, DT_PAD - SSM_HEADS)))
    return pl.pallas_call(
        functools.partial(_ssm_kernel, T=T),
        grid=(B, S // T),
        in_specs=[pl.BlockSpec((None, T, SSM_XBC), lambda b, c: (b, c, (PROJ_W - SSM_XBC) // SSM_XBC)),
                  pl.BlockSpec((None, T, SSM_W), lambda b, c: (b, c, (PROJ_W - SSM_XBC - SSM_W) // SSM_W)),
                  pl.BlockSpec((None, T, DT_PAD), lambda b, c: (b, c, 0)),
                  pl.BlockSpec((SSM_CONV, SSM_XBC), const),
                  pl.BlockSpec((1, SSM_XBC), const),
                  pl.BlockSpec((1, DT_PAD), const),
                  pl.BlockSpec((1, DT_PAD), const),
                  pl.BlockSpec((1, SSM_W), const),
                  pl.BlockSpec((1, SSM_W), const)],
        out_specs=pl.BlockSpec((None, T, SSM_W), lambda b, c: (b, c, 0)),
        out_shape=jax.ShapeDtypeStruct((B, S, SSM_W), BF16),
        scratch_shapes=[pltpu.VMEM((T + 8, SSM_XBC), F32),
                        pltpu.VMEM((SSM_GROUPS, SSM_STATE, SSM_W // SSM_GROUPS), F32)],
        compiler_params=_cparams("arbitrary", "arbitrary"),
        name="ssd",
    )(proj, proj, dt_pad, conv_w, conv_b.reshape(1, SSM_XBC), pad_heads(dt_bias), pad_heads(a_log),
      jnp.repeat(d_skip, SSM_HEAD_DIM).reshape(1, SSM_W), gain.reshape(1, SSM_W))


def _outproj_kernel(yr_ref, ys_ref, ym_ref, h_ref, mod_ref, w_ref, o_ref):
    mixed = (jnp.dot(yr_ref[...], w_ref[0:RET_W, :], preferred_element_type=F32)
             + jnp.dot(ys_ref[...], w_ref[RET_W:RET_W + SB_W, :], preferred_element_type=F32)
             + jnp.dot(ym_ref[...], w_ref[RET_W + SB_W:MIX_W, :], preferred_element_type=F32))
    o_ref[...] = h_ref[...] + (1.0 + mod_ref[2:3, :]) * mixed


def _outproj(y_ret, y_sb, y_ssm, h, mod3, w):
    B, S, D = h.shape
    tm = min(TOKEN_TILE, S)
    tok = lambda width: pl.BlockSpec((None, tm, width), lambda b, i: (b, i, 0))
    return pl.pallas_call(
        _outproj_kernel,
        grid=(B, S // tm),
        in_specs=[tok(RET_W), tok(SB_W), tok(SSM_W), tok(D),
                  pl.BlockSpec((None, 3, D), lambda b, i: (b, 0, 0)),
                  pl.BlockSpec((MIX_W, D), lambda b, i: (0, 0))],
        out_specs=tok(D),
        out_shape=jax.ShapeDtypeStruct((B, S, D), F32),
        compiler_params=_cparams("arbitrary", "arbitrary"),
        name="outproj",
    )(y_ret, y_sb, y_ssm, h, mod3, w)


def _final_kernel(h_ref, mod_ref, gain_ref, o_ref):
    o_ref[...] = _norm_mod(h_ref[...], gain_ref[...], mod_ref[0:1, :], mod_ref[1:2, :])


def _final_norm(h, mod2, gain):
    B, S, D = h.shape
    tm = min(TOKEN_TILE, S)
    return pl.pallas_call(
        _final_kernel,
        grid=(B, S // tm),
        in_specs=[pl.BlockSpec((None, tm, D), lambda b, i: (b, i, 0)),
                  pl.BlockSpec((None, 2, D), lambda b, i: (b, 0, 0)),
                  pl.BlockSpec((1, D), lambda b, i: (0, 0))],
        out_specs=pl.BlockSpec((None, tm, D), lambda b, i: (b, i, 0)),
        out_shape=jax.ShapeDtypeStruct((B, S, D), F32),
        compiler_params=_cparams("arbitrary", "arbitrary"),
        name="final_norm",
    )(h, mod2, gain)


def kernel(x, c, ada_w, ada_b, norm_ffn1, ffn1_wg, ffn1_wu, ffn1_wd, norm_mix, w_in, conv_w, conv_b,
           dt_bias, a_log, d_skip, ret_gn, ssm_norm, w_out, norm_ffn2, ffn2_wg, ffn2_wu, ffn2_wd,
           final_ada_w, final_ada_b, final_norm):
    B, S, D = x.shape
    assert D == D_MODEL and S % CHUNK == 0 and B <= 8
    c_pad = jnp.pad(c, ((0, 8 - B), (0, 0)))
    mod = _ada_proj(c_pad, ada_w, ada_b[:, None, :], 1024)[:, :B].reshape(DEPTH, B, 3 * N_SUB, D)
    fmod = _ada_proj(c_pad, final_ada_w[None], final_ada_b[None, None, :], 1024)[0, :B].reshape(B, 2, D)
    ret_consts = _retention_consts(S)
    bf = lambda a: a.astype(BF16)

    h = x
    for l in range(DEPTH):
        h = _ffn(h, mod[l, :, 0:3], norm_ffn1[l][None], bf(ffn1_wg[l]), bf(ffn1_wu[l]), bf(ffn1_wd[l]))
        w_pad = jnp.pad(bf(w_in[l]), ((0, 0), (0, PROJ_W + DT_PAD - w_in.shape[-1])))
        proj, dt_pad = _inproj(h, mod[l, :, 3:6], norm_mix[l][None], w_pad)
        y_ret = _retention(proj, ret_gn[l].reshape(RET_HEADS, 1, RET_DIM), ret_consts)
        y_sb = _stick_breaking(proj)
        y_ssm = _ssm(proj, dt_pad, conv_w[l], conv_b[l], dt_bias[l], a_log[l], d_skip[l], ssm_norm[l])
        h = _outproj(y_ret, y_sb, y_ssm, h, mod[l, :, 3:6], bf(w_out[l]))
        h = _ffn(h, mod[l, :, 6:9], norm_ffn2[l][None], bf(ffn2_wg[l]), bf(ffn2_wu[l]), bf(ffn2_wd[l]))
    return _final_norm(h, fmod, final_norm[None])
```

```python
import functools

import jax
import jax.numpy as jnp
from jax import lax
from jax.experimental import pallas as pl
from jax.experimental.pallas import tpu as pltpu

D_MODEL = 1024
DEPTH = 2
CHUNK = 64
RET_HEADS = 4
RET_DIM = 128
SB_HEADS = 4
SB_DIM = 128
SSM_HEADS = 8
SSM_HEAD_DIM = 64
SSM_STATE = 128
SSM_GROUPS = 2
SSM_CONV = 4
D_FF = 2816
ROPE_BASE = 10000.0
NORM_EPS = 1e-6
N_SUB = 3

RET_W = RET_HEADS * RET_DIM
SB_W = SB_HEADS * SB_DIM
SSM_W = SSM_HEADS * SSM_HEAD_DIM
MIX_W = RET_W + SB_W + SSM_W
SSM_XBC = SSM_W + 2 * SSM_GROUPS * SSM_STATE
PROJ_W = 4 * RET_W + 3 * SB_W + SSM_W + SSM_XBC
GROUP_W = 512
LANES = 128
DT_PAD = LANES
CONV_PAD = 8

F32 = jnp.float32
BF16 = jnp.bfloat16

TOKEN_TILE = 512
FF_CHUNK = 256
RET_TILE = 256
SB_TILE = 256
SSM_TILE = 256
SB_DEAD_LOG_WEIGHT = -104.0
VMEM_LIMIT = 56 * 1024 * 1024


def _cparams(*sem):
    return pltpu.CompilerParams(dimension_semantics=sem, vmem_limit_bytes=VMEM_LIMIT)


def _silu(x):
    return x / (1.0 + jnp.exp(-x))


def _norm_mod(h, gain, shift, scale):
    y = h * lax.rsqrt(jnp.mean(h * h, axis=-1, keepdims=True) + NORM_EPS) * gain
    return y * (1.0 + scale) + shift


def _ada_kernel(c_ref, w_ref, b_ref, o_ref):
    cond = _silu(c_ref[...])
    o_ref[...] = jnp.dot(cond, w_ref[...], precision=lax.Precision.HIGHEST,
                         preferred_element_type=F32) + b_ref[...]


def _ada_proj(c_pad, w, b, tn):
    L, D, N = w.shape
    return pl.pallas_call(
        _ada_kernel,
        grid=(L, N // tn),
        in_specs=[pl.BlockSpec((8, D), lambda l, j: (0, 0)),
                  pl.BlockSpec((None, D, tn), lambda l, j: (l, 0, j)),
                  pl.BlockSpec((None, 1, tn), lambda l, j: (l, 0, j))],
        out_specs=pl.BlockSpec((None, 8, tn), lambda l, j: (l, 0, j)),
        out_shape=jax.ShapeDtypeStruct((L, 8, N), F32),
        compiler_params=_cparams("arbitrary", "arbitrary"),
        name="ada_proj",
    )(c_pad, w, b)


def _ffn_kernel(h_ref, mod_ref, gain_ref, wg_ref, wu_ref, wd_ref, *rest, final):
    o_ref = rest[-1]
    h = h_ref[...]
    u = _norm_mod(h, gain_ref[...], mod_ref[0:1, :], mod_ref[1:2, :]).astype(BF16)
    acc = jnp.zeros(h.shape, F32)
    for j in range(D_FF // FF_CHUNK):
        cols = slice(j * FF_CHUNK, (j + 1) * FF_CHUNK)
        g = jnp.dot(u, wg_ref[:, cols], preferred_element_type=F32)
        up = jnp.dot(u, wu_ref[:, cols], preferred_element_type=F32)
        a = (_silu(g) * up).astype(BF16)
        acc = acc + jnp.dot(a, wd_ref[cols, :], preferred_element_type=F32)
    out = h + (0.5 * (1.0 + mod_ref[2:3, :])) * acc
    if final:
        fmod_ref, fgain_ref = rest[0], rest[1]
        out = _norm_mod(out, fgain_ref[...], fmod_ref[0:1, :], fmod_ref[1:2, :])
    o_ref[...] = out


def _ffn(h, mod3, gain, wg, wu, wd, final_mod=None, final_gain=None):
    B, S, D = h.shape
    tm = min(TOKEN_TILE, S)
    const = lambda b, i: (0, 0)
    final = final_mod is not None
    extra_specs = [pl.BlockSpec((None, 2, D), lambda b, i: (b, 0, 0)), pl.BlockSpec((1, D), const)] if final else []
    extra = (final_mod, final_gain) if final else ()
    return pl.pallas_call(
        functools.partial(_ffn_kernel, final=final),
        grid=(B, S // tm),
        in_specs=[pl.BlockSpec((None, tm, D), lambda b, i: (b, i, 0)),
                  pl.BlockSpec((None, 3, D), lambda b, i: (b, 0, 0)),
                  pl.BlockSpec((1, D), const),
                  pl.BlockSpec((D, D_FF), const),
                  pl.BlockSpec((D, D_FF), const),
                  pl.BlockSpec((D_FF, D), const)] + extra_specs,
        out_specs=pl.BlockSpec((None, tm, D), lambda b, i: (b, i, 0)),
        out_shape=jax.ShapeDtypeStruct((B, S, D), F32),
        compiler_params=_cparams("arbitrary", "arbitrary"),
        name="ffn",
    )(h, mod3, gain, wg, wu, wd, *extra)


def _inproj_kernel(h_ref, mod_ref, gain_ref, w_ref, cos_ref, sin_ref, convw_ref, convb_ref,
                   proj_ref, dt_ref, xbuf_ref):
    tm = h_ref.shape[0]

    @pl.when(pl.program_id(1) == 0)
    def _():
        xbuf_ref[0:CONV_PAD, :] = jnp.zeros((CONV_PAD, SSM_XBC), F32)

    u = _norm_mod(h_ref[...], gain_ref[...], mod_ref[0:1, :], mod_ref[1:2, :]).astype(BF16)
    cos2 = cos_ref[...]
    sin2 = sin_ref[...]

    def rope(x):
        heads = [x[:, a * RET_DIM:(a + 1) * RET_DIM] for a in range(RET_HEADS)]
        return jnp.concatenate([y * cos2 + pltpu.roll(y, RET_DIM // 2, 1) * sin2 for y in heads], axis=1)

    post = {0: rope,
            1: lambda y: rope(y) * (RET_DIM ** -0.5),
            3: _silu,
            4: lambda y: y * (SB_DIM ** -0.5),
            7: _silu}
    def project(j):
        return jnp.dot(u, w_ref[:, j * GROUP_W:(j + 1) * GROUP_W], preferred_element_type=F32)

    xbc0 = (PROJ_W - SSM_XBC) // GROUP_W
    for j in range(SSM_XBC // GROUP_W):
        xbuf_ref[CONV_PAD:CONV_PAD + tm, j * GROUP_W:(j + 1) * GROUP_W] = project(xbc0 + j)
    dt_ref[...] = jnp.dot(u, w_ref[:, PROJ_W:PROJ_W + DT_PAD], preferred_element_type=F32)
    for j in range(SSM_XBC // GROUP_W):
        cols = slice(j * GROUP_W, (j + 1) * GROUP_W)
        rows = xbuf_ref[:, cols]
        conv = convb_ref[:, cols] + convw_ref[SSM_CONV - 1:SSM_CONV, cols] * rows[CONV_PAD:]
        for back in range(1, SSM_CONV):
            tap = convw_ref[SSM_CONV - 1 - back:SSM_CONV - back, cols]
            conv = conv + tap * pltpu.roll(rows, back, 0)[CONV_PAD:]
        proj_ref[:, (xbc0 + j) * GROUP_W:(xbc0 + j + 1) * GROUP_W] = _silu(conv).astype(proj_ref.dtype)
    xbuf_ref[0:CONV_PAD, :] = xbuf_ref[tm:tm + CONV_PAD, :]
    for j in range(xbc0):
        proj_ref[:, j * GROUP_W:(j + 1) * GROUP_W] = post.get(j, lambda y: y)(project(j)).astype(proj_ref.dtype)


def _inproj(h, mod3, gain, w_pad, cos2, sin2, conv_w, conv_b):
    B, S, D = h.shape
    tm = min(TOKEN_TILE, S)
    const = lambda b, i: (0, 0)
    return pl.pallas_call(
        _inproj_kernel,
        grid=(B, S // tm),
        in_specs=[pl.BlockSpec((None, tm, D), lambda b, i: (b, i, 0)),
                  pl.BlockSpec((None, 3, D), lambda b, i: (b, 0, 0)),
                  pl.BlockSpec((1, D), const),
                  pl.BlockSpec((D, PROJ_W + DT_PAD), const),
                  pl.BlockSpec((tm, RET_DIM), lambda b, i: (i, 0)),
                  pl.BlockSpec((tm, RET_DIM), lambda b, i: (i, 0)),
                  pl.BlockSpec((SSM_CONV, SSM_XBC), const),
                  pl.BlockSpec((1, SSM_XBC), const)],
        out_specs=[pl.BlockSpec((None, tm, PROJ_W), lambda b, i: (b, i, 0)),
                   pl.BlockSpec((None, tm, DT_PAD), lambda b, i: (b, i, 0))],
        out_shape=[jax.ShapeDtypeStruct((B, S, PROJ_W), BF16),
                   jax.ShapeDtypeStruct((B, S, DT_PAD), F32)],
        scratch_shapes=[pltpu.VMEM((tm + CONV_PAD, SSM_XBC), F32)],
        compiler_params=_cparams("arbitrary", "arbitrary"),
        name="inproj",
    )(h, mod3, gain, w_pad, cos2, sin2, conv_w, conv_b.reshape(1, SSM_XBC))


def _rope_tables(S):
    half = RET_DIM // 2
    pos = jnp.arange(S, dtype=F32)
    inv_freq = ROPE_BASE ** (-jnp.arange(half, dtype=F32) / half)
    ang = pos[:, None] * inv_freq[None, :]
    cos, sin = jnp.cos(ang), jnp.sin(ang)
    return jnp.concatenate([cos, cos], axis=-1), jnp.concatenate([-sin, sin], axis=-1)


def _ret_kernel(q_ref, k_ref, v_ref, g_ref, dmat_ref, qdec_ref, kdec_ref, sdec_ref, gn_ref, o_ref, state_ref):
    @pl.when(pl.program_id(1) == 0)
    def _():
        state_ref[...] = jnp.zeros_like(state_ref)

    for h in range(RET_HEADS):
        cols = slice(h * RET_DIM, (h + 1) * RET_DIM)
        qr = q_ref[:, cols]
        kr = k_ref[:, cols]
        v = v_ref[:, cols]
        state = state_ref[h]
        scores = lax.dot_general(qr, kr, (((1,), (1,)), ((), ())), preferred_element_type=F32) * dmat_ref[h]
        y = jnp.dot(scores.astype(BF16), v, preferred_element_type=F32)
        y = y + qdec_ref[h] * jnp.dot(qr, state.astype(BF16), preferred_element_type=F32)
        kv = lax.dot_general(kr, (kdec_ref[h] * v.astype(F32)).astype(BF16), (((0,), (0,)), ((), ())),
                             preferred_element_type=F32)
        state_ref[h] = state * sdec_ref[h, 0:1, :] + kv
        y = y * lax.rsqrt(jnp.mean(y * y, axis=-1, keepdims=True) + NORM_EPS) * gn_ref[h]
        o_ref[:, cols] = (y * g_ref[:, cols].astype(F32)).astype(o_ref.dtype)


def _retention(proj, gn, consts):
    B, S, _ = proj.shape
    T = min(RET_TILE, S)
    H, Dh = RET_HEADS, RET_DIM
    dmat, qdec, kdec, sdec = consts
    group = lambda n: pl.BlockSpec((None, T, GROUP_W), lambda b, c: (b, c, n))
    whole = lambda a: pl.BlockSpec(a.shape, lambda b, c: (0,) * a.ndim)
    return pl.pallas_call(
        _ret_kernel,
        grid=(B, S // T),
        in_specs=[group(0), group(1), group(2), group(3),
                  whole(dmat), whole(qdec), whole(kdec), whole(sdec), whole(gn)],
        out_specs=pl.BlockSpec((None, T, RET_W), lambda b, c: (b, c, 0)),
        out_shape=jax.ShapeDtypeStruct((B, S, RET_W), BF16),
        scratch_shapes=[pltpu.VMEM((H, Dh, Dh), F32)],
        compiler_params=_cparams("arbitrary", "arbitrary"),
        name="retention",
    )(proj, proj, proj, proj, dmat, qdec, kdec, sdec, gn)


def _retention_consts(S):
    T = min(RET_TILE, S)
    H, Dh = RET_HEADS, RET_DIM
    log_gamma = jnp.log1p(-(2.0 ** (-5.0 - jnp.arange(H, dtype=F32))))
    idx = jnp.arange(T, dtype=F32)
    chunk_of = jnp.arange(T) // CHUNK
    seen = chunk_of[None, :] <= chunk_of[:, None]
    dmat = jnp.where(seen[None], jnp.exp(log_gamma[:, None, None] * jnp.abs(idx[:, None] - idx[None, :])), 0.0)
    qdec = jnp.exp(log_gamma[:, None] * (idx + 1.0)[None, :])
    kdec = jnp.exp(log_gamma[:, None] * (T - 1 - idx)[None, :])
    sdec = jnp.exp(log_gamma * T)
    bc = lambda a: jnp.broadcast_to(a[:, :, None], (H, T, Dh))
    return dmat, bc(qdec), bc(kdec), jnp.broadcast_to(sdec[:, None, None], (H, 8, Dh))


def _split2(x):
    hi = x.astype(BF16)
    return hi, (x - hi.astype(F32)).astype(BF16)


def _split3(x):
    hi = x.astype(BF16)
    rest = x - hi.astype(F32)
    mid = rest.astype(BF16)
    return hi, mid, (rest - mid.astype(F32)).astype(BF16)


def _sb_kernel(q_ref, k_ref, v_ref, o_ref, acc_ref, carry_ref, *, tq):
    i = pl.program_id(1)
    row = lax.broadcasted_iota(jnp.int32, (tq, tq), 0)
    col = lax.broadcasted_iota(jnp.int32, (tq, tq), 1)
    visible = col < row
    later = jnp.where(row > col, 1.0, 0.0).astype(BF16)

    heads = range(SB_HEADS)
    head_cols = [slice(h * SB_DIM, (h + 1) * SB_DIM) for h in heads]

    def blocks(j, diagonal):
        rows = pl.ds(pl.multiple_of(j * tq, tq), tq)
        z = [lax.dot_general(q_ref[:, c], k_ref[rows, c], (((1,), (1,)), ((), ())), preferred_element_type=F32)
             for c in head_cols]
        log_beta = [jnp.minimum(x, 0.0) - jnp.log(1.0 + jnp.exp(-jnp.abs(x))) for x in z]
        log_keep = [lb - x for lb, x in zip(log_beta, z)]
        if diagonal:
            log_keep = [jnp.where(visible, lk, 0.0) for lk in log_keep]
        pieces = [_split2(lk) for lk in log_keep]
        tail_in = [jnp.dot(hi, later, preferred_element_type=F32) + jnp.dot(lo, later, preferred_element_type=F32)
                   for hi, lo in pieces]
        total = [t[:, 0:1] + lk[:, 0:1] for t, lk in zip(tail_in, log_keep)]
        if diagonal:
            w = [jnp.where(visible, jnp.exp(lb + t), 0.0) for lb, t in zip(log_beta, tail_in)]
        else:
            carry = [carry_ref[h] for h in heads]
            w = [jnp.exp(lb + t + cy) for lb, t, cy in zip(log_beta, tail_in, carry)]
            total = [cy + tt for cy, tt in zip(carry, total)]
        out = [jnp.dot(w[h].astype(BF16), v_ref[rows, head_cols[h]], preferred_element_type=F32) for h in heads]
        for h in heads:
            carry_ref[h] = total[h]
            if diagonal:
                acc_ref[:, head_cols[h]] = out[h]
            else:
                acc_ref[:, head_cols[h]] += out[h]

    blocks(i, True)

    @pl.when(i > 0)
    def _():
        blocks(i - 1, False)

    def cond(s):
        j, top = s
        return jnp.logical_and(j >= 0, top > SB_DEAD_LOG_WEIGHT)

    def body(s):
        j, _ = s
        blocks(j, False)
        return j - 1, jnp.max(carry_ref[...])

    lax.while_loop(cond, body, (i - 2, jnp.max(carry_ref[...])))
    o_ref[...] = acc_ref[...].astype(o_ref.dtype)


def _stick_breaking(proj):
    B, S, _ = proj.shape
    T = min(SB_TILE, S)
    first = 4 * RET_W // GROUP_W
    return pl.pallas_call(
        functools.partial(_sb_kernel, tq=T),
        grid=(B, S // T),
        in_specs=[pl.BlockSpec((None, T, GROUP_W), lambda b, i: (b, i, first)),
                  pl.BlockSpec((None, S, GROUP_W), lambda b, i: (b, 0, first + 1)),
                  pl.BlockSpec((None, S, GROUP_W), lambda b, i: (b, 0, first + 2))],
        out_specs=pl.BlockSpec((None, T, SB_W), lambda b, i: (b, i, 0)),
        out_shape=jax.ShapeDtypeStruct((B, S, SB_W), BF16),
        scratch_shapes=[pltpu.VMEM((T, SB_W), F32), pltpu.VMEM((SB_HEADS, T, 1), F32)],
        compiler_params=_cparams("arbitrary", "arbitrary"),
        name="stick_breaking",
    )(proj, proj, proj)


def _expand_heads(v):
    T = v.shape[0]
    lane = lax.broadcasted_iota(jnp.int32, (T, LANES), 1)
    tiles = []
    for j in range(SSM_W // LANES):
        a = jnp.broadcast_to(v[:, 2 * j:2 * j + 1], (T, LANES))
        b = jnp.broadcast_to(v[:, 2 * j + 1:2 * j + 2], (T, LANES))
        tiles.append(jnp.where(lane < SSM_HEAD_DIM, a, b))
    return jnp.concatenate(tiles, axis=1)


def _ssm_kernel(xc_ref, z_ref, dt_ref, dtb_ref, alog_ref, dskip_ref, gain_ref, o_ref, state_ref, *, T):
    G, Hg, P, N = SSM_GROUPS, SSM_HEADS // SSM_GROUPS, SSM_HEAD_DIM, SSM_STATE

    @pl.when(pl.program_id(1) == 0)
    def _():
        state_ref[...] = jnp.zeros_like(state_ref)

    xs = xc_ref[:, :SSM_W].astype(F32)
    bm = [xc_ref[:, SSM_W + g * N:SSM_W + (g + 1) * N] for g in range(G)]
    cm = [xc_ref[:, SSM_W + (G + g) * N:SSM_W + (G + g + 1) * N] for g in range(G)]

    dt_in = dt_ref[...] + dtb_ref[...]
    dt = jnp.maximum(dt_in, 0.0) + jnp.log1p(jnp.exp(-jnp.abs(dt_in)))
    da = dt * (-jnp.exp(alog_ref[...]))
    row = lax.broadcasted_iota(jnp.int32, (T, T), 0)
    col = lax.broadcasted_iota(jnp.int32, (T, T), 1)
    causal = row >= col
    ones_below = jnp.where(causal, 1.0, 0.0).astype(BF16)
    acum = sum(jnp.dot(ones_below, piece, preferred_element_type=F32) for piece in _split3(da))
    acum_t = acum.T
    last = acum[T - 1:T, :]
    dt_x = _expand_heads(dt)
    grow_x = _expand_heads(jnp.exp(acum))
    end_x = _expand_heads(jnp.exp(last - acum))
    xdt = xs * dt_x
    xdt_b = xdt.astype(BF16)
    xend_b = (xdt * end_x).astype(BF16)

    group_cols = [slice(g * Hg * P, (g + 1) * Hg * P) for g in range(G)]
    cb = [lax.dot_general(cm[g], bm[g], (((1,), (1,)), ((), ())), preferred_element_type=F32) for g in range(G)]
    state = [state_ref[g] for g in range(G)]
    y_inter = [jnp.dot(cm[g], state[g].astype(BF16), preferred_element_type=F32) * grow_x[:, group_cols[g]]
               for g in range(G)]
    w = [(cb[h // Hg] * jnp.where(causal, jnp.exp(acum[:, h:h + 1] - acum_t[h:h + 1, :]), 0.0)).astype(BF16)
         for h in range(SSM_HEADS)]
    y_intra = [jnp.dot(w[h], xdt_b[:, h * P:(h + 1) * P], preferred_element_type=F32) for h in range(SSM_HEADS)]
    for g in range(G):
        state_ref[g] = state[g] * grow_x[T - 1:T, group_cols[g]] + lax.dot_general(
            bm[g], xend_b[:, group_cols[g]], (((0,), (0,)), ((), ())), preferred_element_type=F32)

    y = jnp.concatenate(y_intra, axis=1) + jnp.concatenate(y_inter, axis=1) + xs * dskip_ref[...]
    y = y * z_ref[...].astype(F32)
    o_ref[...] = (y * lax.rsqrt(jnp.mean(y * y, axis=-1, keepdims=True) + NORM_EPS)
                  * gain_ref[...]).astype(o_ref.dtype)


def _ssm(proj, dt_pad, dt_bias, a_log, d_skip, gain):
    B, S, _ = proj.shape
    T = min(SSM_TILE, S)
    const = lambda b, c: (0, 0)
    pad_heads = lambda a: jnp.pad(a.reshape(1, SSM_HEADS), ((0, 0), (0, DT_PAD - SSM_HEADS)))
    return pl.pallas_call(
        functools.partial(_ssm_kernel, T=T),
        grid=(B, S // T),
        in_specs=[pl.BlockSpec((None, T, SSM_XBC), lambda b, c: (b, c, (PROJ_W - SSM_XBC) // SSM_XBC)),
                  pl.BlockSpec((None, T, SSM_W), lambda b, c: (b, c, (PROJ_W - SSM_XBC - SSM_W) // SSM_W)),
                  pl.BlockSpec((None, T, DT_PAD), lambda b, c: (b, c, 0)),
                  pl.BlockSpec((1, DT_PAD), const),
                  pl.BlockSpec((1, DT_PAD), const),
                  pl.BlockSpec((1, SSM_W), const),
                  pl.BlockSpec((1, SSM_W), const)],
        out_specs=pl.BlockSpec((None, T, SSM_W), lambda b, c: (b, c, 0)),
        out_shape=jax.ShapeDtypeStruct((B, S, SSM_W), BF16),
        scratch_shapes=[pltpu.VMEM((SSM_GROUPS, SSM_STATE, SSM_W // SSM_GROUPS), F32)],
        compiler_params=_cparams("arbitrary", "arbitrary"),
        name="ssd",
    )(proj, proj, dt_pad, pad_heads(dt_bias), pad_heads(a_log),
      jnp.repeat(d_skip, SSM_HEAD_DIM).reshape(1, SSM_W), gain.reshape(1, SSM_W))


def _outproj_kernel(yr_ref, ys_ref, ym_ref, h_ref, mod_ref, w_ref, o_ref):
    mixed = (jnp.dot(yr_ref[...], w_ref[0:RET_W, :], preferred_element_type=F32)
             + jnp.dot(ys_ref[...], w_ref[RET_W:RET_W + SB_W, :], preferred_element_type=F32)
             + jnp.dot(ym_ref[...], w_ref[RET_W + SB_W:MIX_W, :], preferred_element_type=F32))
    o_ref[...] = h_ref[...] + (1.0 + mod_ref[2:3, :]) * mixed


def _outproj(y_ret, y_sb, y_ssm, h, mod3, w):
    B, S, D = h.shape
    tm = min(TOKEN_TILE, S)
    tok = lambda width: pl.BlockSpec((None, tm, width), lambda b, i: (b, i, 0))
    return pl.pallas_call(
        _outproj_kernel,
        grid=(B, S // tm),
        in_specs=[tok(RET_W), tok(SB_W), tok(SSM_W), tok(D),
                  pl.BlockSpec((None, 3, D), lambda b, i: (b, 0, 0)),
                  pl.BlockSpec((MIX_W, D), lambda b, i: (0, 0))],
        out_specs=tok(D),
        out_shape=jax.ShapeDtypeStruct((B, S, D), F32),
        compiler_params=_cparams("arbitrary", "arbitrary"),
        name="outproj",
    )(y_ret, y_sb, y_ssm, h, mod3, w)


def kernel(x, c, ada_w, ada_b, norm_ffn1, ffn1_wg, ffn1_wu, ffn1_wd, norm_mix, w_in, conv_w, conv_b,
           dt_bias, a_log, d_skip, ret_gn, ssm_norm, w_out, norm_ffn2, ffn2_wg, ffn2_wu, ffn2_wd,
           final_ada_w, final_ada_b, final_norm):
    B, S, D = x.shape
    assert D == D_MODEL and S % CHUNK == 0 and B <= 8
    c_pad = jnp.pad(c, ((0, 8 - B), (0, 0)))
    mod = _ada_proj(c_pad, ada_w, ada_b[:, None, :], 1024)[:, :B].reshape(DEPTH, B, 3 * N_SUB, D)
    fmod = _ada_proj(c_pad, final_ada_w[None], final_ada_b[None, None, :], 1024)[0, :B].reshape(B, 2, D)
    cos2, sin2 = _rope_tables(S)
    ret_consts = _retention_consts(S)
    bf = lambda a: a.astype(BF16)

    h = x
    for l in range(DEPTH):
        h = _ffn(h, mod[l, :, 0:3], norm_ffn1[l][None], bf(ffn1_wg[l]), bf(ffn1_wu[l]), bf(ffn1_wd[l]))
        w_pad = jnp.pad(bf(w_in[l]), ((0, 0), (0, PROJ_W + DT_PAD - w_in.shape[-1])))
        proj, dt_pad = _inproj(h, mod[l, :, 3:6], norm_mix[l][None], w_pad, cos2, sin2, conv_w[l], conv_b[l])
        y_ret = _retention(proj, ret_gn[l].reshape(RET_HEADS, 1, RET_DIM), ret_consts)
        y_sb = _stick_breaking(proj)
        y_ssm = _ssm(proj, dt_pad, dt_bias[l], a_log[l], d_skip[l], ssm_norm[l])
        h = _outproj(y_ret, y_sb, y_ssm, h, mod[l, :, 3:6], bf(w_out[l]))
        last = l == DEPTH - 1
        h = _ffn(h, mod[l, :, 6:9], norm_ffn2[l][None], bf(ffn2_wg[l]), bf(ffn2_wu[l]), bf(ffn2_wd[l]),
                 final_mod=fmod if last else None, final_gain=final_norm[None] if last else None)
    return h
```

```python
import functools

import jax
import jax.numpy as jnp
from jax import lax
from jax.experimental import pallas as pl
from jax.experimental.pallas import tpu as pltpu

D_MODEL = 1024
DEPTH = 2
CHUNK = 64
RET_HEADS = 4
RET_DIM = 128
SB_HEADS = 4
SB_DIM = 128
SSM_HEADS = 8
SSM_HEAD_DIM = 64
SSM_STATE = 128
SSM_GROUPS = 2
SSM_CONV = 4
D_FF = 2816
ROPE_BASE = 10000.0
NORM_EPS = 1e-6
N_SUB = 3

RET_W = RET_HEADS * RET_DIM
SB_W = SB_HEADS * SB_DIM
SSM_W = SSM_HEADS * SSM_HEAD_DIM
MIX_W = RET_W + SB_W + SSM_W
SSM_XBC = SSM_W + 2 * SSM_GROUPS * SSM_STATE
PROJ_W = 4 * RET_W + 3 * SB_W + SSM_W + SSM_XBC
GROUP_W = 512
LANES = 128
DT_PAD = LANES
CONV_PAD = 8

F32 = jnp.float32
BF16 = jnp.bfloat16

TOKEN_TILE = 512
FF_CHUNK = 256
CAST_ROWS = 256
MIX_TILE = 256
SB_DEAD_LOG_WEIGHT = -104.0
VMEM_LIMIT = 56 * 1024 * 1024


def _cparams(*sem):
    return pltpu.CompilerParams(dimension_semantics=sem, vmem_limit_bytes=VMEM_LIMIT)


def _silu(x):
    return x / (1.0 + jnp.exp(-x))


def _norm_mod(h, gain, shift, scale):
    y = h * lax.rsqrt(jnp.mean(h * h, axis=-1, keepdims=True) + NORM_EPS) * gain
    return y * (1.0 + scale) + shift


def _ada_kernel(c_ref, w_ref, b_ref, o_ref):
    cond = _silu(c_ref[...])
    o_ref[...] = jnp.dot(cond, w_ref[...], precision=lax.Precision.HIGHEST,
                         preferred_element_type=F32) + b_ref[...]


def _ada_proj(c_pad, w, b, tn):
    L, D, N = w.shape
    return pl.pallas_call(
        _ada_kernel,
        grid=(L, N // tn),
        in_specs=[pl.BlockSpec((8, D), lambda l, j: (0, 0)),
                  pl.BlockSpec((None, D, tn), lambda l, j: (l, 0, j)),
                  pl.BlockSpec((None, 1, tn), lambda l, j: (l, 0, j))],
        out_specs=pl.BlockSpec((None, 8, tn), lambda l, j: (l, 0, j)),
        out_shape=jax.ShapeDtypeStruct((L, 8, N), F32),
        compiler_params=_cparams("arbitrary", "arbitrary"),
        name="ada_proj",
    )(c_pad, w, b)


def _cast_kernel(w_ref, o_ref):
    o_ref[...] = w_ref[...].astype(o_ref.dtype)


def _to_bf16(w, cols=None):
    L, R, C = w.shape
    cols = C if cols is None else cols
    tr = CAST_ROWS
    assert R % tr == 0 and (cols == C or cols % LANES == 0)
    return pl.pallas_call(
        _cast_kernel,
        grid=(L, R // tr),
        in_specs=[pl.BlockSpec((None, tr, cols), lambda l, r: (l, r, 0))],
        out_specs=pl.BlockSpec((None, tr, cols), lambda l, r: (l, r, 0)),
        out_shape=jax.ShapeDtypeStruct((L, R, cols), BF16),
        compiler_params=_cparams("arbitrary", "arbitrary"),
        name="to_bf16",
    )(w)


def _ffn_kernel(h_ref, mod_ref, gain_ref, wg_ref, wu_ref, wd_ref, *rest, final):
    o_ref = rest[-1]
    h = h_ref[...]
    u = _norm_mod(h, gain_ref[...], mod_ref[0:1, :], mod_ref[1:2, :]).astype(BF16)
    acc = jnp.zeros(h.shape, F32)
    for j in range(D_FF // FF_CHUNK):
        cols = slice(j * FF_CHUNK, (j + 1) * FF_CHUNK)
        g = jnp.dot(u, wg_ref[:, cols], preferred_element_type=F32)
        up = jnp.dot(u, wu_ref[:, cols], preferred_element_type=F32)
        a = (_silu(g) * up).astype(BF16)
        acc = acc + jnp.dot(a, wd_ref[cols, :], preferred_element_type=F32)
    out = h + (0.5 * (1.0 + mod_ref[2:3, :])) * acc
    if final:
        fmod_ref, fgain_ref = rest[0], rest[1]
        out = _norm_mod(out, fgain_ref[...], fmod_ref[0:1, :], fmod_ref[1:2, :])
    o_ref[...] = out


def _ffn(h, mod3, gain, layer, wg, wu, wd, final_mod=None, final_gain=None):
    B, S, D = h.shape
    tm = min(TOKEN_TILE, S)
    const = lambda b, i: (0, 0)
    of_layer = lambda b, i: (layer, 0, 0)
    final = final_mod is not None
    extra_specs = [pl.BlockSpec((None, 2, D), lambda b, i: (b, 0, 0)), pl.BlockSpec((1, D), const)] if final else []
    extra = (final_mod, final_gain) if final else ()
    return pl.pallas_call(
        functools.partial(_ffn_kernel, final=final),
        grid=(B, S // tm),
        in_specs=[pl.BlockSpec((None, tm, D), lambda b, i: (b, i, 0)),
                  pl.BlockSpec((None, 3, D), lambda b, i: (b, 0, 0)),
                  pl.BlockSpec((1, D), const),
                  pl.BlockSpec((None, D, D_FF), of_layer),
                  pl.BlockSpec((None, D, D_FF), of_layer),
                  pl.BlockSpec((None, D_FF, D), of_layer)] + extra_specs,
        out_specs=pl.BlockSpec((None, tm, D), lambda b, i: (b, i, 0)),
        out_shape=jax.ShapeDtypeStruct((B, S, D), F32),
        compiler_params=_cparams("arbitrary", "arbitrary"),
        name="ffn",
    )(h, mod3, gain, wg, wu, wd, *extra)


def _inproj_kernel(h_ref, mod_ref, gain_ref, w_ref, wdt_ref, cos_ref, sin_ref, convw_ref, convb_ref,
                   proj_ref, dt_ref, xbuf_ref):
    tm = h_ref.shape[0]

    @pl.when(pl.program_id(1) == 0)
    def _():
        xbuf_ref[0:CONV_PAD, :] = jnp.zeros((CONV_PAD, SSM_XBC), F32)

    u = _norm_mod(h_ref[...], gain_ref[...], mod_ref[0:1, :], mod_ref[1:2, :]).astype(BF16)
    cos2 = cos_ref[...]
    sin2 = sin_ref[...]

    def rope(x):
        heads = [x[:, a * RET_DIM:(a + 1) * RET_DIM] for a in range(RET_HEADS)]
        return jnp.concatenate([y * cos2 + pltpu.roll(y, RET_DIM // 2, 1) * sin2 for y in heads], axis=1)

    post = {0: rope,
            1: lambda y: rope(y) * (RET_DIM ** -0.5),
            3: _silu,
            4: lambda y: y * (SB_DIM ** -0.5),
            7: _silu}
    def project(j):
        return jnp.dot(u, w_ref[:, j * GROUP_W:(j + 1) * GROUP_W], preferred_element_type=F32)

    xbc0 = (PROJ_W - SSM_XBC) // GROUP_W
    for j in range(SSM_XBC // GROUP_W):
        xbuf_ref[CONV_PAD:CONV_PAD + tm, j * GROUP_W:(j + 1) * GROUP_W] = project(xbc0 + j)
    dt_ref[...] = jnp.dot(u, wdt_ref[...], preferred_element_type=F32)
    for j in range(SSM_XBC // GROUP_W):
        cols = slice(j * GROUP_W, (j + 1) * GROUP_W)
        rows = xbuf_ref[:, cols]
        conv = convb_ref[:, cols] + convw_ref[SSM_CONV - 1:SSM_CONV, cols] * rows[CONV_PAD:]
        for back in range(1, SSM_CONV):
            tap = convw_ref[SSM_CONV - 1 - back:SSM_CONV - back, cols]
            conv = conv + tap * pltpu.roll(rows, back, 0)[CONV_PAD:]
        proj_ref[:, (xbc0 + j) * GROUP_W:(xbc0 + j + 1) * GROUP_W] = _silu(conv).astype(proj_ref.dtype)
    xbuf_ref[0:CONV_PAD, :] = xbuf_ref[tm:tm + CONV_PAD, :]
    for j in range(xbc0):
        proj_ref[:, j * GROUP_W:(j + 1) * GROUP_W] = post.get(j, lambda y: y)(project(j)).astype(proj_ref.dtype)


def _inproj(h, mod3, gain, layer, w_main, w_dt, cos2, sin2, conv_w, conv_b):
    B, S, D = h.shape
    tm = min(TOKEN_TILE, S)
    const = lambda b, i: (0, 0)
    of_layer = lambda b, i: (layer, 0, 0)
    return pl.pallas_call(
        _inproj_kernel,
        grid=(B, S // tm),
        in_specs=[pl.BlockSpec((None, tm, D), lambda b, i: (b, i, 0)),
                  pl.BlockSpec((None, 3, D), lambda b, i: (b, 0, 0)),
                  pl.BlockSpec((1, D), const),
                  pl.BlockSpec((None, D, PROJ_W), of_layer),
                  pl.BlockSpec((None, D, DT_PAD), of_layer),
                  pl.BlockSpec((tm, RET_DIM), lambda b, i: (i, 0)),
                  pl.BlockSpec((tm, RET_DIM), lambda b, i: (i, 0)),
                  pl.BlockSpec((SSM_CONV, SSM_XBC), const),
                  pl.BlockSpec((1, SSM_XBC), const)],
        out_specs=[pl.BlockSpec((None, tm, PROJ_W), lambda b, i: (b, i, 0)),
                   pl.BlockSpec((None, tm, DT_PAD), lambda b, i: (b, i, 0))],
        out_shape=[jax.ShapeDtypeStruct((B, S, PROJ_W), BF16),
                   jax.ShapeDtypeStruct((B, S, DT_PAD), F32)],
        scratch_shapes=[pltpu.VMEM((tm + CONV_PAD, SSM_XBC), F32)],
        compiler_params=_cparams("arbitrary", "arbitrary"),
        name="inproj",
    )(h, mod3, gain, w_main, w_dt, cos2, sin2, conv_w, conv_b.reshape(1, SSM_XBC))


def _rope_tables(S):
    half = RET_DIM // 2
    pos = jnp.arange(S, dtype=F32)
    inv_freq = ROPE_BASE ** (-jnp.arange(half, dtype=F32) / half)
    ang = pos[:, None] * inv_freq[None, :]
    cos, sin = jnp.cos(ang), jnp.sin(ang)
    return jnp.concatenate([cos, cos], axis=-1), jnp.concatenate([-sin, sin], axis=-1)


def _run(stages):
    return _interleave(stages)[0]


def _interleave(*staged):
    values = [None] * len(staged)
    live = list(enumerate(staged))
    while live:
        still = []
        for n, stages in live:
            try:
                next(stages)
                still.append((n, stages))
            except StopIteration as done:
                values[n] = done.value
        live = still
    return values


def _delayed(stages, rounds):
    for _ in range(rounds):
        yield
    return (yield from stages)


def _ret_tile(q_ref, k_ref, v_ref, g_ref, dmat_ref, qdec_ref, kdec_ref, sdec_ref, gn_ref, state_ref):
    heads = range(RET_HEADS)
    cols = [slice(h * RET_DIM, (h + 1) * RET_DIM) for h in heads]
    nt = (((1,), (1,)), ((), ()))
    tn = (((0,), (0,)), ((), ()))
    state = [state_ref[h] for h in heads]
    scores = [lax.dot_general(q_ref[:, c], k_ref[:, c], nt, preferred_element_type=F32) for c in cols]
    cross = [jnp.dot(q_ref[:, cols[h]], state[h].astype(BF16), preferred_element_type=F32) for h in heads]
    yield
    vdec = [(kdec_ref[h] * v_ref[:, cols[h]].astype(F32)).astype(BF16) for h in heads]
    kv = [lax.dot_general(k_ref[:, cols[h]], vdec[h], tn, preferred_element_type=F32) for h in heads]
    yield
    weights = [(scores[h] * dmat_ref[h]).astype(BF16) for h in heads]
    yield
    y = [jnp.dot(weights[h], v_ref[:, cols[h]], preferred_element_type=F32) + qdec_ref[h] * cross[h] for h in heads]
    yield
    outs = []
    for h in heads:
        state_ref[h] = state[h] * sdec_ref[h, 0:1, :] + kv[h]
        yn = y[h] * lax.rsqrt(jnp.mean(y[h] * y[h], axis=-1, keepdims=True) + NORM_EPS) * gn_ref[h]
        outs.append((yn * g_ref[:, cols[h]].astype(F32)).astype(BF16))
    return outs


def _retention_consts(S):
    T = min(MIX_TILE, S)
    H, Dh = RET_HEADS, RET_DIM
    log_gamma = jnp.log1p(-(2.0 ** (-5.0 - jnp.arange(H, dtype=F32))))
    idx = jnp.arange(T, dtype=F32)
    chunk_of = jnp.arange(T) // CHUNK
    seen = chunk_of[None, :] <= chunk_of[:, None]
    dmat = jnp.where(seen[None], jnp.exp(log_gamma[:, None, None] * jnp.abs(idx[:, None] - idx[None, :])), 0.0)
    qdec = jnp.exp(log_gamma[:, None] * (idx + 1.0)[None, :])
    kdec = jnp.exp(log_gamma[:, None] * (T - 1 - idx)[None, :])
    sdec = jnp.exp(log_gamma * T)
    bc = lambda a: jnp.broadcast_to(a[:, :, None], (H, T, Dh))
    return dmat, bc(qdec), bc(kdec), jnp.broadcast_to(sdec[:, None, None], (H, 8, Dh))


def _split2(x):
    hi = x.astype(BF16)
    return hi, (x - hi.astype(F32)).astype(BF16)


def _split3(x):
    hi = x.astype(BF16)
    rest = x - hi.astype(F32)
    mid = rest.astype(BF16)
    return hi, mid, (rest - mid.astype(F32)).astype(BF16)


SB_STAGES = 5
SB_CARRY_STAGE = 4


def _sb_blocks(q_ref, k_ref, v_ref, acc_ref, carry_ref, j, diagonal, tq, exists=None):
    row = lax.broadcasted_iota(jnp.int32, (tq, tq), 0)
    col = lax.broadcasted_iota(jnp.int32, (tq, tq), 1)
    visible = col < row
    later = jnp.where(row > col, 1.0, 0.0).astype(BF16)
    heads = range(SB_HEADS)
    head_cols = [slice(h * SB_DIM, (h + 1) * SB_DIM) for h in heads]

    rows = pl.ds(pl.multiple_of(j * tq, tq), tq)
    z = [lax.dot_general(q_ref[:, c], k_ref[rows, c], (((1,), (1,)), ((), ())), preferred_element_type=F32)
         for c in head_cols]
    yield
    log_beta = [jnp.minimum(x, 0.0) - jnp.log(1.0 + jnp.exp(-jnp.abs(x))) for x in z]
    log_keep = [lb - x for lb, x in zip(log_beta, z)]
    if diagonal:
        log_keep = [jnp.where(visible, lk, 0.0) for lk in log_keep]
    if exists is not None:
        log_keep = [jnp.where(exists, lk, 0.0) for lk in log_keep]
    pieces = [_split2(lk) for lk in log_keep]
    yield
    tail_in = [jnp.dot(hi, later, preferred_element_type=F32) + jnp.dot(lo, later, preferred_element_type=F32)
               for hi, lo in pieces]
    yield
    total = [t[:, 0:1] + lk[:, 0:1] for t, lk in zip(tail_in, log_keep)]
    if diagonal:
        w = [jnp.where(visible, jnp.exp(lb + t), 0.0) for lb, t in zip(log_beta, tail_in)]
    else:
        carry = [carry_ref[h] for h in heads]
        w = [jnp.exp(lb + t + cy) for lb, t, cy in zip(log_beta, tail_in, carry)]
        total = [cy + tt for cy, tt in zip(carry, total)]
    if exists is not None:
        w = [jnp.where(exists, x, 0.0) for x in w]
    yield
    out = [jnp.dot(w[h].astype(BF16), v_ref[rows, head_cols[h]], preferred_element_type=F32) for h in heads]
    for h in heads:
        carry_ref[h] = total[h]
        if diagonal:
            acc_ref[:, head_cols[h]] = out[h]
        else:
            acc_ref[:, head_cols[h]] += out[h]


def _sb_earlier_blocks(q_ref, k_ref, v_ref, acc_ref, carry_ref, i, tq):
    refs = (q_ref, k_ref, v_ref, acc_ref, carry_ref)

    def cond(s):
        j, top = s
        return jnp.logical_and(j >= 0, top > SB_DEAD_LOG_WEIGHT)

    def body(s):
        j, _ = s
        _run(_sb_blocks(*refs, j, False, tq))
        return j - 1, jnp.max(carry_ref[...])

    lax.while_loop(cond, body, (i - 2, jnp.max(carry_ref[...])))


def _expand_heads(v):
    T = v.shape[0]
    lane = lax.broadcasted_iota(jnp.int32, (T, LANES), 1)
    tiles = []
    for j in range(SSM_W // LANES):
        a = jnp.broadcast_to(v[:, 2 * j:2 * j + 1], (T, LANES))
        b = jnp.broadcast_to(v[:, 2 * j + 1:2 * j + 2], (T, LANES))
        tiles.append(jnp.where(lane < SSM_HEAD_DIM, a, b))
    return jnp.concatenate(tiles, axis=1)


def _ssm_tile(xc_ref, z_ref, dt_ref, dtb_ref, alog_ref, dskip_ref, gain_ref, state_ref, T):
    G, Hg, P, N = SSM_GROUPS, SSM_HEADS // SSM_GROUPS, SSM_HEAD_DIM, SSM_STATE
    xs = xc_ref[:, :SSM_W].astype(F32)
    bm = [xc_ref[:, SSM_W + g * N:SSM_W + (g + 1) * N] for g in range(G)]
    cm = [xc_ref[:, SSM_W + (G + g) * N:SSM_W + (G + g + 1) * N] for g in range(G)]

    dt_in = dt_ref[...] + dtb_ref[...]
    dt = jnp.maximum(dt_in, 0.0) + jnp.log1p(jnp.exp(-jnp.abs(dt_in)))
    da = dt * (-jnp.exp(alog_ref[...]))
    row = lax.broadcasted_iota(jnp.int32, (T, T), 0)
    col = lax.broadcasted_iota(jnp.int32, (T, T), 1)
    causal = row >= col
    ones_below = jnp.where(causal, 1.0, 0.0).astype(BF16)
    acum = sum(jnp.dot(ones_below, piece, preferred_element_type=F32) for piece in _split3(da))
    group_cols = [slice(g * Hg * P, (g + 1) * Hg * P) for g in range(G)]
    cb = [lax.dot_general(cm[g], bm[g], (((1,), (1,)), ((), ())), preferred_element_type=F32) for g in range(G)]
    state = [state_ref[g] for g in range(G)]
    yield
    acum_t = acum.T
    last = acum[T - 1:T, :]
    dt_x = _expand_heads(dt)
    grow_x = _expand_heads(jnp.exp(acum))
    end_x = _expand_heads(jnp.exp(last - acum))
    xdt = xs * dt_x
    xdt_b = xdt.astype(BF16)
    xend_b = (xdt * end_x).astype(BF16)
    yield
    y_inter = [jnp.dot(cm[g], state[g].astype(BF16), preferred_element_type=F32) * grow_x[:, group_cols[g]]
               for g in range(G)]
    w = []
    for h in range(SSM_HEADS):
        w.append((cb[h // Hg] * jnp.where(causal, jnp.exp(acum[:, h:h + 1] - acum_t[h:h + 1, :]), 0.0)).astype(BF16))
        if h % 2 == 1:
            yield
    y_intra = [jnp.dot(w[h], xdt_b[:, h * P:(h + 1) * P], preferred_element_type=F32) for h in range(SSM_HEADS)]
    for g in range(G):
        state_ref[g] = state[g] * grow_x[T - 1:T, group_cols[g]] + lax.dot_general(
            bm[g], xend_b[:, group_cols[g]], (((0,), (0,)), ((), ())), preferred_element_type=F32)
    yield
    y = jnp.concatenate(y_intra, axis=1) + jnp.concatenate(y_inter, axis=1) + xs * dskip_ref[...]
    y = y * z_ref[...].astype(F32)
    return (y * lax.rsqrt(jnp.mean(y * y, axis=-1, keepdims=True) + NORM_EPS) * gain_ref[...]).astype(BF16)


def _mixer_kernel(rq_ref, rk_ref, rv_ref, rg_ref, sq_ref, sk_ref, sv_ref, mz_ref, mxc_ref, dt_ref,
                  h_ref, mod_ref, wout_ref, dmat_ref, qdec_ref, kdec_ref, sdec_ref, gn_ref,
                  dtb_ref, alog_ref, dskip_ref, sgain_ref,
                  o_ref, ret_state_ref, ssm_state_ref, sb_acc_ref, sb_carry_ref, mixed_ref, *, T):
    i = pl.program_id(1)

    @pl.when(i == 0)
    def _():
        ret_state_ref[...] = jnp.zeros_like(ret_state_ref)
        ssm_state_ref[...] = jnp.zeros_like(ssm_state_ref)

    sb_refs = (sq_ref, sk_ref, sv_ref, sb_acc_ref, sb_carry_ref)
    _, y_ssm, y_ret, _ = _interleave(
        _sb_blocks(*sb_refs, i, True, T),
        _ssm_tile(mxc_ref, mz_ref, dt_ref, dtb_ref, alog_ref, dskip_ref, sgain_ref, ssm_state_ref, T),
        _ret_tile(rq_ref, rk_ref, rv_ref, rg_ref, dmat_ref, qdec_ref, kdec_ref, sdec_ref, gn_ref, ret_state_ref),
        _delayed(_sb_blocks(*sb_refs, jnp.maximum(i - 1, 0), False, T, exists=i > 0), SB_STAGES - SB_CARRY_STAGE + 1))
    mixed = jnp.dot(y_ssm, wout_ref[RET_W + SB_W:MIX_W, :], preferred_element_type=F32)
    for hd in range(RET_HEADS):
        mixed = mixed + jnp.dot(y_ret[hd], wout_ref[hd * RET_DIM:(hd + 1) * RET_DIM, :], preferred_element_type=F32)
    mixed_ref[...] = mixed

    _sb_earlier_blocks(*sb_refs, i, T)
    mixed = mixed_ref[...] + jnp.dot(sb_acc_ref[...].astype(BF16), wout_ref[RET_W:RET_W + SB_W, :],
                                     preferred_element_type=F32)
    o_ref[...] = h_ref[...] + (1.0 + mod_ref[2:3, :]) * mixed


def _mixer(proj, dt_pad, h, mod3, layer, w_out, ret_consts, gn, dt_bias, a_log, d_skip, sgain):
    B, S, D = h.shape
    T = min(MIX_TILE, S)
    dmat, qdec, kdec, sdec = ret_consts
    group = lambda n: pl.BlockSpec((None, T, GROUP_W), lambda b, i: (b, i, n))
    whole_seq = lambda n: pl.BlockSpec((None, S, GROUP_W), lambda b, i: (b, 0, n), pipeline_mode=pl.Buffered(1))
    whole = lambda a: pl.BlockSpec(a.shape, lambda b, i: (0,) * a.ndim)
    pad_heads = lambda a: jnp.pad(a.reshape(1, SSM_HEADS), ((0, 0), (0, DT_PAD - SSM_HEADS)))
    small = (dmat, qdec, kdec, sdec, gn, pad_heads(dt_bias), pad_heads(a_log),
             jnp.repeat(d_skip, SSM_HEAD_DIM).reshape(1, SSM_W), sgain.reshape(1, SSM_W))
    xbc_block = (PROJ_W - SSM_XBC) // SSM_XBC
    return pl.pallas_call(
        functools.partial(_mixer_kernel, T=T),
        grid=(B, S // T),
        in_specs=[group(0), group(1), group(2), group(3),
                  group(4), whole_seq(5), whole_seq(6),
                  group(7),
                  pl.BlockSpec((None, T, SSM_XBC), lambda b, i: (b, i, xbc_block)),
                  pl.BlockSpec((None, T, DT_PAD), lambda b, i: (b, i, 0)),
                  pl.BlockSpec((None, T, D), lambda b, i: (b, i, 0)),
                  pl.BlockSpec((None, 3, D), lambda b, i: (b, 0, 0)),
                  pl.BlockSpec((None, MIX_W, D), lambda b, i: (layer, 0, 0))] + [whole(a) for a in small],
        out_specs=pl.BlockSpec((None, T, D), lambda b, i: (b, i, 0)),
        out_shape=jax.ShapeDtypeStruct((B, S, D), F32),
        scratch_shapes=[pltpu.VMEM((RET_HEADS, RET_DIM, RET_DIM), F32),
                        pltpu.VMEM((SSM_GROUPS, SSM_STATE, SSM_W // SSM_GROUPS), F32),
                        pltpu.VMEM((T, SB_W), F32),
                        pltpu.VMEM((SB_HEADS, T, 1), F32),
                        pltpu.VMEM((T, D), F32)],
        compiler_params=_cparams("arbitrary", "arbitrary"),
        name="mixer",
    )(proj, proj, proj, proj, proj, proj, proj, proj, proj, dt_pad, h, mod3, w_out, *small)


def kernel(x, c, ada_w, ada_b, norm_ffn1, ffn1_wg, ffn1_wu, ffn1_wd, norm_mix, w_in, conv_w, conv_b,
           dt_bias, a_log, d_skip, ret_gn, ssm_norm, w_out, norm_ffn2, ffn2_wg, ffn2_wu, ffn2_wd,
           final_ada_w, final_ada_b, final_norm):
    B, S, D = x.shape
    assert D == D_MODEL and S % CHUNK == 0 and B <= 8
    c_pad = jnp.pad(c, ((0, 8 - B), (0, 0)))
    mod = _ada_proj(c_pad, ada_w, ada_b[:, None, :], 3 * N_SUB * D // 4)[:, :B].reshape(DEPTH, B, 3 * N_SUB, D)
    fmod = _ada_proj(c_pad, final_ada_w[None], final_ada_b[None, None, :], 1024)[0, :B].reshape(B, 2, D)
    cos2, sin2 = _rope_tables(S)
    ret_consts = _retention_consts(S)
    ffn1_w = [_to_bf16(w) for w in (ffn1_wg, ffn1_wu, ffn1_wd)]
    ffn2_w = [_to_bf16(w) for w in (ffn2_wg, ffn2_wu, ffn2_wd)]
    w_main = _to_bf16(w_in, PROJ_W)
    w_dt = jnp.pad(w_in[:, :, PROJ_W:].astype(BF16), ((0, 0), (0, 0), (0, DT_PAD - SSM_HEADS)))
    w_out_b = _to_bf16(w_out)

    h = x
    for l in range(DEPTH):
        h = _ffn(h, mod[l, :, 0:3], norm_ffn1[l][None], l, *ffn1_w)
        proj, dt_pad = _inproj(h, mod[l, :, 3:6], norm_mix[l][None], l, w_main, w_dt, cos2, sin2,
                               conv_w[l], conv_b[l])
        h = _mixer(proj, dt_pad, h, mod[l, :, 3:6], l, w_out_b, ret_consts,
                   ret_gn[l].reshape(RET_HEADS, 1, RET_DIM), dt_bias[l], a_log[l], d_skip[l], ssm_norm[l])
        last = l == DEPTH - 1
        h = _ffn(h, mod[l, :, 6:9], norm_ffn2[l][None], l, *ffn2_w,
                 final_mod=fmod if last else None, final_gain=final_norm[None] if last else None)
    return h
```

```python
import functools

import jax
import jax.numpy as jnp
from jax import lax
from jax.experimental import pallas as pl
from jax.experimental.pallas import tpu as pltpu

D_MODEL = 1024
DEPTH = 2
CHUNK = 64
RET_HEADS = 4
RET_DIM = 128
SB_HEADS = 4
SB_DIM = 128
SSM_HEADS = 8
SSM_HEAD_DIM = 64
SSM_STATE = 128
SSM_GROUPS = 2
SSM_CONV = 4
D_FF = 2816
ROPE_BASE = 10000.0
NORM_EPS = 1e-6
N_SUB = 3

RET_W = RET_HEADS * RET_DIM
SB_W = SB_HEADS * SB_DIM
SSM_W = SSM_HEADS * SSM_HEAD_DIM
MIX_W = RET_W + SB_W + SSM_W
SSM_XBC = SSM_W + 2 * SSM_GROUPS * SSM_STATE
PROJ_W = 4 * RET_W + 3 * SB_W + SSM_W + SSM_XBC
GROUP_W = 512
LANES = 128
DT_PAD = LANES
CONV_PAD = 8

F32 = jnp.float32
BF16 = jnp.bfloat16

TOKEN_TILE = 512
FFN_TILE = 1024
FF_CHUNK = 256
CAST_ROWS = 256
PROJ_CHUNK = 256
MIX_TILE = 256
SB_DEAD_LOG_WEIGHT = -104.0
VMEM_LIMIT = 56 * 1024 * 1024


def _cparams(*sem):
    return pltpu.CompilerParams(dimension_semantics=sem, vmem_limit_bytes=VMEM_LIMIT)


def _silu(x):
    return x / (1.0 + jnp.exp(-x))


def _norm_mod(h, gain, shift, scale):
    y = h * lax.rsqrt(jnp.mean(h * h, axis=-1, keepdims=True) + NORM_EPS) * gain
    return y * (1.0 + scale) + shift


def _ada_kernel(c_ref, w_ref, b_ref, o_ref):
    cond = _silu(c_ref[...])
    o_ref[...] = jnp.dot(cond, w_ref[...], precision=lax.Precision.HIGHEST,
                         preferred_element_type=F32) + b_ref[...]


def _ada_proj(c_pad, w, b, tn):
    L, D, N = w.shape
    return pl.pallas_call(
        _ada_kernel,
        grid=(L, N // tn),
        in_specs=[pl.BlockSpec((8, D), lambda l, j: (0, 0)),
                  pl.BlockSpec((None, D, tn), lambda l, j: (l, 0, j)),
                  pl.BlockSpec((None, 1, tn), lambda l, j: (l, 0, j))],
        out_specs=pl.BlockSpec((None, 8, tn), lambda l, j: (l, 0, j)),
        out_shape=jax.ShapeDtypeStruct((L, 8, N), F32),
        compiler_params=_cparams("arbitrary", "arbitrary"),
        name="ada_proj",
    )(c_pad, w, b)


def _cast_kernel(w_ref, o_ref):
    o_ref[...] = w_ref[...].astype(o_ref.dtype)


def _to_bf16(w, cols=None):
    L, R, C = w.shape
    cols = C if cols is None else cols
    tr = CAST_ROWS
    assert R % tr == 0 and (cols == C or cols % LANES == 0)
    return pl.pallas_call(
        _cast_kernel,
        grid=(L, R // tr),
        in_specs=[pl.BlockSpec((None, tr, cols), lambda l, r: (l, r, 0))],
        out_specs=pl.BlockSpec((None, tr, cols), lambda l, r: (l, r, 0)),
        out_shape=jax.ShapeDtypeStruct((L, R, cols), BF16),
        compiler_params=_cparams("arbitrary", "arbitrary"),
        name="to_bf16",
    )(w)


def _ffn_kernel(h_ref, mod_ref, gain_ref, wg_ref, wu_ref, wd_ref, *rest, final):
    o_ref = rest[-1]
    h = h_ref[...]
    u = _norm_mod(h, gain_ref[...], mod_ref[0:1, :], mod_ref[1:2, :]).astype(BF16)
    acc = jnp.zeros(h.shape, F32)
    for j in range(D_FF // FF_CHUNK):
        cols = slice(j * FF_CHUNK, (j + 1) * FF_CHUNK)
        g = jnp.dot(u, wg_ref[:, cols], preferred_element_type=F32)
        up = jnp.dot(u, wu_ref[:, cols], preferred_element_type=F32)
        a = (_silu(g) * up).astype(BF16)
        acc = acc + jnp.dot(a, wd_ref[cols, :], preferred_element_type=F32)
    out = h + (0.5 * (1.0 + mod_ref[2:3, :])) * acc
    if final:
        fmod_ref, fgain_ref = rest[0], rest[1]
        out = _norm_mod(out, fgain_ref[...], fmod_ref[0:1, :], fmod_ref[1:2, :])
    o_ref[...] = out


def _ffn(h, mod3, gain, layer, wg, wu, wd, final_mod=None, final_gain=None):
    B, S, D = h.shape
    tm = min(FFN_TILE, S)
    const = lambda b, i: (0, 0)
    of_layer = lambda b, i: (layer, 0, 0)
    final = final_mod is not None
    extra_specs = [pl.BlockSpec((None, 2, D), lambda b, i: (b, 0, 0)), pl.BlockSpec((1, D), const)] if final else []
    extra = (final_mod, final_gain) if final else ()
    return pl.pallas_call(
        functools.partial(_ffn_kernel, final=final),
        grid=(B, S // tm),
        in_specs=[pl.BlockSpec((None, tm, D), lambda b, i: (b, i, 0)),
                  pl.BlockSpec((None, 3, D), lambda b, i: (b, 0, 0)),
                  pl.BlockSpec((1, D), const),
                  pl.BlockSpec((None, D, D_FF), of_layer),
                  pl.BlockSpec((None, D, D_FF), of_layer),
                  pl.BlockSpec((None, D_FF, D), of_layer)] + extra_specs,
        out_specs=pl.BlockSpec((None, tm, D), lambda b, i: (b, i, 0)),
        out_shape=jax.ShapeDtypeStruct((B, S, D), F32),
        compiler_params=_cparams("arbitrary", "arbitrary"),
        name="ffn",
    )(h, mod3, gain, wg, wu, wd, *extra)


def _inproj_kernel(h_ref, mod_ref, gain_ref, w_ref, wdt_ref, cos_ref, sin_ref, convw_ref, convb_ref,
                   proj_ref, dt_ref, xbuf_ref):
    tm = h_ref.shape[0]

    @pl.when(pl.program_id(1) == 0)
    def _():
        xbuf_ref[0:CONV_PAD, :] = jnp.zeros((CONV_PAD, SSM_XBC), F32)

    u = _norm_mod(h_ref[...], gain_ref[...], mod_ref[0:1, :], mod_ref[1:2, :]).astype(BF16)
    cos2 = cos_ref[...]
    sin2 = sin_ref[...]

    def rope(x):
        heads = [x[:, a * RET_DIM:(a + 1) * RET_DIM] for a in range(x.shape[1] // RET_DIM)]
        return jnp.concatenate([y * cos2 + pltpu.roll(y, RET_DIM // 2, 1) * sin2 for y in heads], axis=1)

    post = {0: rope,
            1: lambda y: rope(y) * (RET_DIM ** -0.5),
            3: _silu,
            4: lambda y: y * (SB_DIM ** -0.5),
            7: _silu}
    def project(j):
        return jnp.dot(u, w_ref[:, j * GROUP_W:(j + 1) * GROUP_W], preferred_element_type=F32)

    xbc0 = (PROJ_W - SSM_XBC) // GROUP_W
    for j in range(SSM_XBC // GROUP_W):
        xbuf_ref[CONV_PAD:CONV_PAD + tm, j * GROUP_W:(j + 1) * GROUP_W] = project(xbc0 + j)
    dt_ref[...] = jnp.dot(u, wdt_ref[...], preferred_element_type=F32)
    for j in range(SSM_XBC // GROUP_W):
        cols = slice(j * GROUP_W, (j + 1) * GROUP_W)
        rows = xbuf_ref[:, cols]
        conv = convb_ref[:, cols] + convw_ref[SSM_CONV - 1:SSM_CONV, cols] * rows[CONV_PAD:]
        for back in range(1, SSM_CONV):
            tap = convw_ref[SSM_CONV - 1 - back:SSM_CONV - back, cols]
            conv = conv + tap * pltpu.roll(rows, back, 0)[CONV_PAD:]
        proj_ref[:, (xbc0 + j) * GROUP_W:(xbc0 + j + 1) * GROUP_W] = _silu(conv).astype(proj_ref.dtype)
    xbuf_ref[0:CONV_PAD, :] = xbuf_ref[tm:tm + CONV_PAD, :]
    for j in range(xbc0):
        for part in range(GROUP_W // PROJ_CHUNK):
            cols = slice(j * GROUP_W + part * PROJ_CHUNK, j * GROUP_W + (part + 1) * PROJ_CHUNK)
            y = jnp.dot(u, w_ref[:, cols], preferred_element_type=F32)
            proj_ref[:, cols] = post.get(j, lambda y: y)(y).astype(proj_ref.dtype)


def _inproj(h, mod3, gain, layer, w_main, w_dt, cos2, sin2, conv_w, conv_b):
    B, S, D = h.shape
    tm = min(TOKEN_TILE, S)
    const = lambda b, i: (0, 0)
    of_layer = lambda b, i: (layer, 0, 0)
    return pl.pallas_call(
        _inproj_kernel,
        grid=(B, S // tm),
        in_specs=[pl.BlockSpec((None, tm, D), lambda b, i: (b, i, 0)),
                  pl.BlockSpec((None, 3, D), lambda b, i: (b, 0, 0)),
                  pl.BlockSpec((1, D), const),
                  pl.BlockSpec((None, D, PROJ_W), of_layer),
                  pl.BlockSpec((None, D, DT_PAD), of_layer),
                  pl.BlockSpec((tm, RET_DIM), lambda b, i: (i, 0)),
                  pl.BlockSpec((tm, RET_DIM), lambda b, i: (i, 0)),
                  pl.BlockSpec((SSM_CONV, SSM_XBC), const),
                  pl.BlockSpec((1, SSM_XBC), const)],
        out_specs=[pl.BlockSpec((None, tm, PROJ_W), lambda b, i: (b, i, 0)),
                   pl.BlockSpec((None, tm, DT_PAD), lambda b, i: (b, i, 0))],
        out_shape=[jax.ShapeDtypeStruct((B, S, PROJ_W), BF16),
                   jax.ShapeDtypeStruct((B, S, DT_PAD), F32)],
        scratch_shapes=[pltpu.VMEM((tm + CONV_PAD, SSM_XBC), F32)],
        compiler_params=_cparams("arbitrary", "arbitrary"),
        name="inproj",
    )(h, mod3, gain, w_main, w_dt, cos2, sin2, conv_w, conv_b.reshape(1, SSM_XBC))


def _rope_tables(S):
    half = RET_DIM // 2
    pos = jnp.arange(S, dtype=F32)
    inv_freq = ROPE_BASE ** (-jnp.arange(half, dtype=F32) / half)
    ang = pos[:, None] * inv_freq[None, :]
    cos, sin = jnp.cos(ang), jnp.sin(ang)
    return jnp.concatenate([cos, cos], axis=-1), jnp.concatenate([-sin, sin], axis=-1)


def _run(stages):
    return _interleave(stages)[0]


def _interleave(*staged):
    values = [None] * len(staged)
    live = list(enumerate(staged))
    while live:
        still = []
        for n, stages in live:
            try:
                next(stages)
                still.append((n, stages))
            except StopIteration as done:
                values[n] = done.value
        live = still
    return values


def _delayed(stages, rounds):
    for _ in range(rounds):
        yield
    return (yield from stages)


def _ret_tile(q_ref, k_ref, v_ref, g_ref, dmat_ref, qdec_ref, kdec_ref, sdec_ref, gn_ref, state_ref, wout_ref):
    heads = range(RET_HEADS)
    cols = [slice(h * RET_DIM, (h + 1) * RET_DIM) for h in heads]
    nt = (((1,), (1,)), ((), ()))
    tn = (((0,), (0,)), ((), ()))
    state = [state_ref[h] for h in heads]
    scores = [lax.dot_general(q_ref[:, c], k_ref[:, c], nt, preferred_element_type=F32) for c in cols]
    cross = [jnp.dot(q_ref[:, cols[h]], state[h].astype(BF16), preferred_element_type=F32) for h in heads]
    yield
    vdec = [(kdec_ref[h] * v_ref[:, cols[h]].astype(F32)).astype(BF16) for h in heads]
    kv = [lax.dot_general(k_ref[:, cols[h]], vdec[h], tn, preferred_element_type=F32) for h in heads]
    yield
    weights = [(scores[h] * dmat_ref[h]).astype(BF16) for h in heads]
    yield
    y = [jnp.dot(weights[h], v_ref[:, cols[h]], preferred_element_type=F32) + qdec_ref[h] * cross[h] for h in heads]
    yield
    outs = []
    for h in heads:
        state_ref[h] = state[h] * sdec_ref[h, 0:1, :] + kv[h]
        yn = y[h] * lax.rsqrt(jnp.mean(y[h] * y[h], axis=-1, keepdims=True) + NORM_EPS) * gn_ref[h]
        outs.append((yn * g_ref[:, cols[h]].astype(F32)).astype(BF16))
    yield
    return sum(jnp.dot(outs[h], wout_ref[cols[h], :], preferred_element_type=F32) for h in heads)


def _retention_consts(S):
    T = min(MIX_TILE, S)
    H, Dh = RET_HEADS, RET_DIM
    log_gamma = jnp.log1p(-(2.0 ** (-5.0 - jnp.arange(H, dtype=F32))))
    idx = jnp.arange(T, dtype=F32)
    chunk_of = jnp.arange(T) // CHUNK
    seen = chunk_of[None, :] <= chunk_of[:, None]
    dmat = jnp.where(seen[None], jnp.exp(log_gamma[:, None, None] * jnp.abs(idx[:, None] - idx[None, :])), 0.0)
    qdec = jnp.exp(log_gamma[:, None] * (idx + 1.0)[None, :])
    kdec = jnp.exp(log_gamma[:, None] * (T - 1 - idx)[None, :])
    sdec = jnp.exp(log_gamma * T)
    bc = lambda a: jnp.broadcast_to(a[:, :, None], (H, T, Dh))
    return dmat, bc(qdec), bc(kdec), jnp.broadcast_to(sdec[:, None, None], (H, 8, Dh))


def _split2(x):
    hi = x.astype(BF16)
    return hi, (x - hi.astype(F32)).astype(BF16)


def _split3(x):
    hi = x.astype(BF16)
    rest = x - hi.astype(F32)
    mid = rest.astype(BF16)
    return hi, mid, (rest - mid.astype(F32)).astype(BF16)


SB_STAGES = 5
SB_CARRY_STAGE = 4


def _sb_blocks(q_ref, k_ref, v_ref, acc_ref, carry_ref, j, diagonal, tq, exists=None):
    row = lax.broadcasted_iota(jnp.int32, (tq, tq), 0)
    col = lax.broadcasted_iota(jnp.int32, (tq, tq), 1)
    visible = col < row
    later = jnp.where(row > col, 1.0, 0.0).astype(BF16)
    heads = range(SB_HEADS)
    head_cols = [slice(h * SB_DIM, (h + 1) * SB_DIM) for h in heads]

    rows = pl.ds(pl.multiple_of(j * tq, tq), tq)
    z = [lax.dot_general(q_ref[:, c], k_ref[rows, c], (((1,), (1,)), ((), ())), preferred_element_type=F32)
         for c in head_cols]
    yield
    log_beta = [jnp.minimum(x, 0.0) - jnp.log(1.0 + jnp.exp(-jnp.abs(x))) for x in z]
    log_keep = [lb - x for lb, x in zip(log_beta, z)]
    if diagonal:
        log_keep = [jnp.where(visible, lk, 0.0) for lk in log_keep]
    if exists is not None:
        log_keep = [jnp.where(exists, lk, 0.0) for lk in log_keep]
    pieces = [_split2(lk) for lk in log_keep]
    yield
    tail_in = [jnp.dot(hi, later, preferred_element_type=F32) + jnp.dot(lo, later, preferred_element_type=F32)
               for hi, lo in pieces]
    yield
    total = [t[:, 0:1] + lk[:, 0:1] for t, lk in zip(tail_in, log_keep)]
    if diagonal:
        w = [jnp.where(visible, jnp.exp(lb + t), 0.0) for lb, t in zip(log_beta, tail_in)]
    else:
        carry = [carry_ref[h] for h in heads]
        w = [jnp.exp(lb + t + cy) for lb, t, cy in zip(log_beta, tail_in, carry)]
        total = [cy + tt for cy, tt in zip(carry, total)]
    if exists is not None:
        w = [jnp.where(exists, x, 0.0) for x in w]
    yield
    out = [jnp.dot(w[h].astype(BF16), v_ref[rows, head_cols[h]], preferred_element_type=F32) for h in heads]
    for h in heads:
        carry_ref[h] = total[h]
        if diagonal:
            acc_ref[:, head_cols[h]] = out[h]
        else:
            acc_ref[:, head_cols[h]] += out[h]


def _sb_earlier_blocks(q_ref, k_ref, v_ref, acc_ref, carry_ref, i, tq):
    refs = (q_ref, k_ref, v_ref, acc_ref, carry_ref)

    def cond(s):
        j, top = s
        return jnp.logical_and(j >= 0, top > SB_DEAD_LOG_WEIGHT)

    def body(s):
        j, _ = s
        _run(_sb_blocks(*refs, j, False, tq))
        return j - 1, jnp.max(carry_ref[...])

    j_end, _ = lax.while_loop(cond, body, (i - 2, jnp.max(carry_ref[...])))
    return i - 2 - j_end


def _expand_heads(v):
    T = v.shape[0]
    lane = lax.broadcasted_iota(jnp.int32, (T, LANES), 1)
    tiles = []
    for j in range(SSM_W // LANES):
        a = jnp.broadcast_to(v[:, 2 * j:2 * j + 1], (T, LANES))
        b = jnp.broadcast_to(v[:, 2 * j + 1:2 * j + 2], (T, LANES))
        tiles.append(jnp.where(lane < SSM_HEAD_DIM, a, b))
    return jnp.concatenate(tiles, axis=1)


def _ssm_tile(xc_ref, z_ref, dt_ref, dtb_ref, alog_ref, dskip_ref, gain_ref, state_ref, wout_ref, T):
    G, Hg, P, N = SSM_GROUPS, SSM_HEADS // SSM_GROUPS, SSM_HEAD_DIM, SSM_STATE
    xs = xc_ref[:, :SSM_W].astype(F32)
    bm = [xc_ref[:, SSM_W + g * N:SSM_W + (g + 1) * N] for g in range(G)]
    cm = [xc_ref[:, SSM_W + (G + g) * N:SSM_W + (G + g + 1) * N] for g in range(G)]

    dt_in = dt_ref[...] + dtb_ref[...]
    dt = jnp.maximum(dt_in, 0.0) + jnp.log1p(jnp.exp(-jnp.abs(dt_in)))
    da = dt * (-jnp.exp(alog_ref[...]))
    row = lax.broadcasted_iota(jnp.int32, (T, T), 0)
    col = lax.broadcasted_iota(jnp.int32, (T, T), 1)
    causal = row >= col
    ones_below = jnp.where(causal, 1.0, 0.0).astype(BF16)
    acum = sum(jnp.dot(ones_below, piece, preferred_element_type=F32) for piece in _split3(da))
    group_cols = [slice(g * Hg * P, (g + 1) * Hg * P) for g in range(G)]
    cb = [lax.dot_general(cm[g], bm[g], (((1,), (1,)), ((), ())), preferred_element_type=F32) for g in range(G)]
    state = [state_ref[g] for g in range(G)]
    yield
    acum_t = acum.T
    last = acum[T - 1:T, :]
    dt_x = _expand_heads(dt)
    grow_x = _expand_heads(jnp.exp(acum))
    end_x = _expand_heads(jnp.exp(last - acum))
    xdt = xs * dt_x
    xdt_b = xdt.astype(BF16)
    xend_b = (xdt * end_x).astype(BF16)
    yield
    y_inter = [jnp.dot(cm[g], state[g].astype(BF16), preferred_element_type=F32) * grow_x[:, group_cols[g]]
               for g in range(G)]
    w = []
    for h in range(SSM_HEADS):
        w.append((cb[h // Hg] * jnp.where(causal, jnp.exp(acum[:, h:h + 1] - acum_t[h:h + 1, :]), 0.0)).astype(BF16))
        if h % 2 == 1:
            yield
    y_intra = [jnp.dot(w[h], xdt_b[:, h * P:(h + 1) * P], preferred_element_type=F32) for h in range(SSM_HEADS)]
    for g in range(G):
        state_ref[g] = state[g] * grow_x[T - 1:T, group_cols[g]] + lax.dot_general(
            bm[g], xend_b[:, group_cols[g]], (((0,), (0,)), ((), ())), preferred_element_type=F32)
    yield
    y = jnp.concatenate(y_intra, axis=1) + jnp.concatenate(y_inter, axis=1) + xs * dskip_ref[...]
    y = y * z_ref[...].astype(F32)
    y = (y * lax.rsqrt(jnp.mean(y * y, axis=-1, keepdims=True) + NORM_EPS) * gain_ref[...]).astype(BF16)
    yield
    return jnp.dot(y, wout_ref[...], preferred_element_type=F32)


def _mixer_kernel(proj_ref, sk_ref, sv_ref, dt_ref,
                  h_ref, mod_ref, wout_ref, dmat_ref, qdec_ref, kdec_ref, sdec_ref, gn_ref,
                  dtb_ref, alog_ref, dskip_ref, sgain_ref,
                  o_ref, ret_state_ref, ssm_state_ref, sb_acc_ref, sb_carry_ref, mixed_ref, *, T):
    i = pl.program_id(1)
    group = lambda n: proj_ref.at[:, n * GROUP_W:(n + 1) * GROUP_W]
    rq_ref, rk_ref, rv_ref, rg_ref, sq_ref, mz_ref = group(0), group(1), group(2), group(3), group(4), group(7)
    mxc_ref = proj_ref.at[:, PROJ_W - SSM_XBC:PROJ_W]

    @pl.when(i == 0)
    def _():
        ret_state_ref[...] = jnp.zeros_like(ret_state_ref)
        ssm_state_ref[...] = jnp.zeros_like(ssm_state_ref)

    sb_refs = (sq_ref, sk_ref, sv_ref, sb_acc_ref, sb_carry_ref)
    _, mixed_ssm, mixed_ret, _ = _interleave(
        _sb_blocks(*sb_refs, i, True, T),
        _ssm_tile(mxc_ref, mz_ref, dt_ref, dtb_ref, alog_ref, dskip_ref, sgain_ref, ssm_state_ref,
                  wout_ref.at[RET_W + SB_W:MIX_W], T),
        _ret_tile(rq_ref, rk_ref, rv_ref, rg_ref, dmat_ref, qdec_ref, kdec_ref, sdec_ref, gn_ref, ret_state_ref,
                  wout_ref.at[0:RET_W]),
        _delayed(_sb_blocks(*sb_refs, jnp.maximum(i - 1, 0), False, T, exists=i > 0), SB_STAGES - SB_CARRY_STAGE + 1))
    mixed_ref[...] = mixed_ssm + mixed_ret

    def finish():
        mixed = mixed_ref[...] + jnp.dot(sb_acc_ref[...].astype(BF16), wout_ref[RET_W:RET_W + SB_W, :],
                                         preferred_element_type=F32)
        o_ref[...] = h_ref[...] + (1.0 + mod_ref[2:3, :]) * mixed

    finish()
    walked = _sb_earlier_blocks(*sb_refs, i, T)
    pl.when(walked > 0)(finish)


def _mixer(proj, dt_pad, h, mod3, layer, w_out, ret_consts, gn, dt_bias, a_log, d_skip, sgain):
    B, S, D = h.shape
    T = min(MIX_TILE, S)
    dmat, qdec, kdec, sdec = ret_consts
    whole_seq = lambda n: pl.BlockSpec((None, S, GROUP_W), lambda b, i: (b, 0, n), pipeline_mode=pl.Buffered(1))
    whole = lambda a: pl.BlockSpec(a.shape, lambda b, i: (0,) * a.ndim)
    pad_heads = lambda a: jnp.pad(a.reshape(1, SSM_HEADS), ((0, 0), (0, DT_PAD - SSM_HEADS)))
    small = (dmat, qdec, kdec, sdec, gn, pad_heads(dt_bias), pad_heads(a_log),
             jnp.repeat(d_skip, SSM_HEAD_DIM).reshape(1, SSM_W), sgain.reshape(1, SSM_W))
    return pl.pallas_call(
        functools.partial(_mixer_kernel, T=T),
        grid=(B, S // T),
        in_specs=[pl.BlockSpec((None, T, PROJ_W), lambda b, i: (b, i, 0)),
                  whole_seq(5), whole_seq(6),
                  pl.BlockSpec((None, T, DT_PAD), lambda b, i: (b, i, 0)),
                  pl.BlockSpec((None, T, D), lambda b, i: (b, i, 0)),
                  pl.BlockSpec((None, 3, D), lambda b, i: (b, 0, 0)),
                  pl.BlockSpec((None, MIX_W, D), lambda b, i: (layer, 0, 0))] + [whole(a) for a in small],
        out_specs=pl.BlockSpec((None, T, D), lambda b, i: (b, i, 0)),
        out_shape=jax.ShapeDtypeStruct((B, S, D), F32),
        scratch_shapes=[pltpu.VMEM((RET_HEADS, RET_DIM, RET_DIM), F32),
                        pltpu.VMEM((SSM_GROUPS, SSM_STATE, SSM_W // SSM_GROUPS), F32),
                        pltpu.VMEM((T, SB_W), F32),
                        pltpu.VMEM((SB_HEADS, T, 1), F32),
                        pltpu.VMEM((T, D), F32)],
        compiler_params=_cparams("arbitrary", "arbitrary"),
        name="mixer",
    )(proj, proj, proj, dt_pad, h, mod3, w_out, *small)


def kernel(x, c, ada_w, ada_b, norm_ffn1, ffn1_wg, ffn1_wu, ffn1_wd, norm_mix, w_in, conv_w, conv_b,
           dt_bias, a_log, d_skip, ret_gn, ssm_norm, w_out, norm_ffn2, ffn2_wg, ffn2_wu, ffn2_wd,
           final_ada_w, final_ada_b, final_norm):
    B, S, D = x.shape
    assert D == D_MODEL and S % CHUNK == 0 and B <= 8
    c_pad = jnp.pad(c, ((0, 8 - B), (0, 0)))
    mod = _ada_proj(c_pad, ada_w, ada_b[:, None, :], 3 * N_SUB * D // 4)[:, :B].reshape(DEPTH, B, 3 * N_SUB, D)
    fmod = _ada_proj(c_pad, final_ada_w[None], final_ada_b[None, None, :], 1024)[0, :B].reshape(B, 2, D)
    cos2, sin2 = _rope_tables(S)
    ret_consts = _retention_consts(S)
    ffn1_w = [_to_bf16(w) for w in (ffn1_wg, ffn1_wu, ffn1_wd)]
    ffn2_w = [_to_bf16(w) for w in (ffn2_wg, ffn2_wu, ffn2_wd)]
    w_main = w_in.astype(BF16)
    w_dt = jnp.pad(w_main[:, :, PROJ_W:], ((0, 0), (0, 0), (0, DT_PAD - SSM_HEADS)))
    w_out_b = _to_bf16(w_out)

    h = x
    for l in range(DEPTH):
        h = _ffn(h, mod[l, :, 0:3], norm_ffn1[l][None], l, *ffn1_w)
        proj, dt_pad = _inproj(h, mod[l, :, 3:6], norm_mix[l][None], l, w_main, w_dt, cos2, sin2,
                               conv_w[l], conv_b[l])
        h = _mixer(proj, dt_pad, h, mod[l, :, 3:6], l, w_out_b, ret_consts,
                   ret_gn[l].reshape(RET_HEADS, 1, RET_DIM), dt_bias[l], a_log[l], d_skip[l], ssm_norm[l])
        last = l == DEPTH - 1
        h = _ffn(h, mod[l, :, 6:9], norm_ffn2[l][None], l, *ffn2_w,
                 final_mod=fmod if last else None, final_gain=final_norm[None] if last else None)
    return h
```

```python
import functools

import jax
import jax.numpy as jnp
from jax import lax
from jax.experimental import pallas as pl
from jax.experimental.pallas import tpu as pltpu

D_MODEL = 1024
DEPTH = 2
CHUNK = 64
RET_HEADS = 4
RET_DIM = 128
SB_HEADS = 4
SB_DIM = 128
SSM_HEADS = 8
SSM_HEAD_DIM = 64
SSM_STATE = 128
SSM_GROUPS = 2
SSM_CONV = 4
D_FF = 2816
ROPE_BASE = 10000.0
NORM_EPS = 1e-6
N_SUB = 3

RET_W = RET_HEADS * RET_DIM
SB_W = SB_HEADS * SB_DIM
SSM_W = SSM_HEADS * SSM_HEAD_DIM
MIX_W = RET_W + SB_W + SSM_W
SSM_XBC = SSM_W + 2 * SSM_GROUPS * SSM_STATE
PROJ_W = 4 * RET_W + 3 * SB_W + SSM_W + SSM_XBC
GROUP_W = 512
LANES = 128
DT_PAD = LANES
CONV_PAD = 8

F32 = jnp.float32
BF16 = jnp.bfloat16

TOKEN_TILE = 1024
FFN_TILE = 1024
FF_CHUNK = 256
CAST_ROWS = 256
MIX_TILE = 256
SB_DEAD_LOG_WEIGHT = -104.0
VMEM_LIMIT = 56 * 1024 * 1024


def _cparams(*sem):
    return pltpu.CompilerParams(dimension_semantics=sem, vmem_limit_bytes=VMEM_LIMIT)


def _silu(x):
    return x / (1.0 + jnp.exp(-x))


def _norm_mod(h, gain, shift, scale):
    y = h * lax.rsqrt(jnp.mean(h * h, axis=-1, keepdims=True) + NORM_EPS) * gain
    return y * (1.0 + scale) + shift


def _ada_kernel(c_ref, w_ref, b_ref, o_ref):
    cond = _silu(c_ref[...])
    o_ref[...] = jnp.dot(cond, w_ref[...], precision=lax.Precision.HIGHEST,
                         preferred_element_type=F32) + b_ref[...]


def _ada_proj(c_pad, w, b, tn):
    L, D, N = w.shape
    return pl.pallas_call(
        _ada_kernel,
        grid=(L, N // tn),
        in_specs=[pl.BlockSpec((8, D), lambda l, j: (0, 0)),
                  pl.BlockSpec((None, D, tn), lambda l, j: (l, 0, j)),
                  pl.BlockSpec((None, 1, tn), lambda l, j: (l, 0, j))],
        out_specs=pl.BlockSpec((None, 8, tn), lambda l, j: (l, 0, j)),
        out_shape=jax.ShapeDtypeStruct((L, 8, N), F32),
        compiler_params=_cparams("arbitrary", "arbitrary"),
        name="ada_proj",
    )(c_pad, w, b)


def _cast_kernel(w_ref, o_ref):
    o_ref[...] = w_ref[...].astype(o_ref.dtype)


def _to_bf16(w, cols=None):
    L, R, C = w.shape
    cols = C if cols is None else cols
    tr = CAST_ROWS
    assert R % tr == 0 and (cols == C or cols % LANES == 0)
    return pl.pallas_call(
        _cast_kernel,
        grid=(L, R // tr),
        in_specs=[pl.BlockSpec((None, tr, cols), lambda l, r: (l, r, 0))],
        out_specs=pl.BlockSpec((None, tr, cols), lambda l, r: (l, r, 0)),
        out_shape=jax.ShapeDtypeStruct((L, R, cols), BF16),
        compiler_params=_cparams("arbitrary", "arbitrary"),
        name="to_bf16",
    )(w)


def _ffn_kernel(h_ref, mod_ref, gain_ref, wg_ref, wu_ref, wd_ref, *rest, final):
    o_ref = rest[-1]
    h = h_ref[...]
    u = _norm_mod(h, gain_ref[...], mod_ref[0:1, :], mod_ref[1:2, :]).astype(BF16)
    acc = jnp.zeros(h.shape, F32)
    for j in range(D_FF // FF_CHUNK):
        cols = slice(j * FF_CHUNK, (j + 1) * FF_CHUNK)
        g = jnp.dot(u, wg_ref[:, cols], preferred_element_type=F32)
        up = jnp.dot(u, wu_ref[:, cols], preferred_element_type=F32)
        a = (_silu(g) * up).astype(BF16)
        acc = acc + jnp.dot(a, wd_ref[cols, :], preferred_element_type=F32)
    out = h + (0.5 * (1.0 + mod_ref[2:3, :])) * acc
    if final:
        fmod_ref, fgain_ref = rest[0], rest[1]
        out = _norm_mod(out, fgain_ref[...], fmod_ref[0:1, :], fmod_ref[1:2, :])
    o_ref[...] = out


def _ffn(h, mod3, gain, layer, wg, wu, wd, final_mod=None, final_gain=None):
    B, S, D = h.shape
    tm = min(FFN_TILE, S)
    const = lambda b, i: (0, 0)
    of_layer = lambda b, i: (layer, 0, 0)
    final = final_mod is not None
    extra_specs = [pl.BlockSpec((None, 2, D), lambda b, i: (b, 0, 0)), pl.BlockSpec((1, D), const)] if final else []
    extra = (final_mod, final_gain) if final else ()
    return pl.pallas_call(
        functools.partial(_ffn_kernel, final=final),
        grid=(B, S // tm),
        in_specs=[pl.BlockSpec((None, tm, D), lambda b, i: (b, i, 0)),
                  pl.BlockSpec((None, 3, D), lambda b, i: (b, 0, 0)),
                  pl.BlockSpec((1, D), const),
                  pl.BlockSpec((None, D, D_FF), of_layer),
                  pl.BlockSpec((None, D, D_FF), of_layer),
                  pl.BlockSpec((None, D_FF, D), of_layer)] + extra_specs,
        out_specs=pl.BlockSpec((None, tm, D), lambda b, i: (b, i, 0)),
        out_shape=jax.ShapeDtypeStruct((B, S, D), F32),
        compiler_params=_cparams("arbitrary", "arbitrary"),
        name="ffn",
    )(h, mod3, gain, wg, wu, wd, *extra)


def _inproj_kernel(h_ref, mod_ref, gain_ref, w_ref, wdt_ref, cos_ref, sin_ref, convw_ref, convb_ref,
                   proj_ref, dt_ref, xbuf_ref):
    tm = h_ref.shape[0]

    @pl.when(pl.program_id(1) == 0)
    def _():
        xbuf_ref[0:CONV_PAD, :] = jnp.zeros((CONV_PAD, SSM_XBC), F32)

    u = _norm_mod(h_ref[...], gain_ref[...], mod_ref[0:1, :], mod_ref[1:2, :]).astype(BF16)
    cos2 = cos_ref[...]
    sin2 = sin_ref[...]

    def rope(x):
        heads = [x[:, a * RET_DIM:(a + 1) * RET_DIM] for a in range(x.shape[1] // RET_DIM)]
        return jnp.concatenate([y * cos2 + pltpu.roll(y, RET_DIM // 2, 1) * sin2 for y in heads], axis=1)

    post = {0: rope,
            1: lambda y: rope(y) * (RET_DIM ** -0.5),
            3: _silu,
            4: lambda y: y * (SB_DIM ** -0.5),
            7: _silu}
    def project(j):
        return jnp.dot(u, w_ref[:, j * GROUP_W:(j + 1) * GROUP_W], preferred_element_type=F32)

    xbc0 = (PROJ_W - SSM_XBC) // GROUP_W
    for j in range(SSM_XBC // GROUP_W):
        xbuf_ref[CONV_PAD:CONV_PAD + tm, j * GROUP_W:(j + 1) * GROUP_W] = project(xbc0 + j)
    dt_ref[...] = jnp.dot(u, wdt_ref[...], preferred_element_type=F32)
    for j in range(SSM_XBC // GROUP_W):
        cols = slice(j * GROUP_W, (j + 1) * GROUP_W)
        rows = xbuf_ref[:, cols]
        conv = convb_ref[:, cols] + convw_ref[SSM_CONV - 1:SSM_CONV, cols] * rows[CONV_PAD:]
        for back in range(1, SSM_CONV):
            tap = convw_ref[SSM_CONV - 1 - back:SSM_CONV - back, cols]
            conv = conv + tap * pltpu.roll(rows, back, 0)[CONV_PAD:]
        proj_ref[:, (xbc0 + j) * GROUP_W:(xbc0 + j + 1) * GROUP_W] = _silu(conv).astype(proj_ref.dtype)
    xbuf_ref[0:CONV_PAD, :] = xbuf_ref[tm:tm + CONV_PAD, :]
    for j in range(xbc0):
        cols = slice(j * GROUP_W, (j + 1) * GROUP_W)
        proj_ref[:, cols] = post.get(j, lambda y: y)(project(j)).astype(proj_ref.dtype)


def _inproj(h, mod3, gain, layer, w_main, w_dt, cos2, sin2, conv_w, conv_b):
    B, S, D = h.shape
    tm = min(TOKEN_TILE, S)
    const = lambda b, i: (0, 0)
    of_layer = lambda b, i: (layer, 0, 0)
    return pl.pallas_call(
        _inproj_kernel,
        grid=(B, S // tm),
        in_specs=[pl.BlockSpec((None, tm, D), lambda b, i: (b, i, 0)),
                  pl.BlockSpec((None, 3, D), lambda b, i: (b, 0, 0)),
                  pl.BlockSpec((1, D), const),
                  pl.BlockSpec((None, D, PROJ_W), of_layer),
                  pl.BlockSpec((None, D, DT_PAD), of_layer),
                  pl.BlockSpec((tm, RET_DIM), lambda b, i: (i, 0)),
                  pl.BlockSpec((tm, RET_DIM), lambda b, i: (i, 0)),
                  pl.BlockSpec((SSM_CONV, SSM_XBC), const),
                  pl.BlockSpec((1, SSM_XBC), const)],
        out_specs=[pl.BlockSpec((None, tm, PROJ_W), lambda b, i: (b, i, 0)),
                   pl.BlockSpec((None, tm, DT_PAD), lambda b, i: (b, i, 0))],
        out_shape=[jax.ShapeDtypeStruct((B, S, PROJ_W), BF16),
                   jax.ShapeDtypeStruct((B, S, DT_PAD), F32)],
        scratch_shapes=[pltpu.VMEM((tm + CONV_PAD, SSM_XBC), F32)],
        compiler_params=_cparams("arbitrary", "arbitrary"),
        name="inproj",
    )(h, mod3, gain, w_main, w_dt, cos2, sin2, conv_w, conv_b.reshape(1, SSM_XBC))


def _rope_tables(S):
    half = RET_DIM // 2
    pos = jnp.arange(S, dtype=F32)
    inv_freq = ROPE_BASE ** (-jnp.arange(half, dtype=F32) / half)
    ang = pos[:, None] * inv_freq[None, :]
    cos, sin = jnp.cos(ang), jnp.sin(ang)
    return jnp.concatenate([cos, cos], axis=-1), jnp.concatenate([-sin, sin], axis=-1)


def _run(stages):
    return _interleave(stages)[0]


def _interleave(*staged):
    values = [None] * len(staged)
    live = list(enumerate(staged))
    while live:
        still = []
        for n, stages in live:
            try:
                next(stages)
                still.append((n, stages))
            except StopIteration as done:
                values[n] = done.value
        live = still
    return values


def _delayed(stages, rounds):
    for _ in range(rounds):
        yield
    return (yield from stages)


def _ret_tile(q_ref, k_ref, v_ref, g_ref, dmat_ref, qdec_ref, kdec_ref, sdec_ref, gn_ref, state_ref, wout_ref):
    heads = range(RET_HEADS)
    cols = [slice(h * RET_DIM, (h + 1) * RET_DIM) for h in heads]
    nt = (((1,), (1,)), ((), ()))
    tn = (((0,), (0,)), ((), ()))
    state = [state_ref[h] for h in heads]
    scores = [lax.dot_general(q_ref[:, c], k_ref[:, c], nt, preferred_element_type=F32) for c in cols]
    cross = [jnp.dot(q_ref[:, cols[h]], state[h].astype(BF16), preferred_element_type=F32) for h in heads]
    yield
    vdec = [(kdec_ref[h] * v_ref[:, cols[h]].astype(F32)).astype(BF16) for h in heads]
    kv = [lax.dot_general(k_ref[:, cols[h]], vdec[h], tn, preferred_element_type=F32) for h in heads]
    yield
    weights = [(scores[h] * dmat_ref[h]).astype(BF16) for h in heads]
    yield
    y = [jnp.dot(weights[h], v_ref[:, cols[h]], preferred_element_type=F32) + qdec_ref[h] * cross[h] for h in heads]
    yield
    outs = []
    for h in heads:
        state_ref[h] = state[h] * sdec_ref[h, 0:1, :] + kv[h]
        yn = y[h] * lax.rsqrt(jnp.mean(y[h] * y[h], axis=-1, keepdims=True) + NORM_EPS) * gn_ref[h]
        outs.append((yn * g_ref[:, cols[h]].astype(F32)).astype(BF16))
    yield
    return sum(jnp.dot(outs[h], wout_ref[cols[h], :], preferred_element_type=F32) for h in heads)


def _retention_consts(S):
    T = min(MIX_TILE, S)
    H, Dh = RET_HEADS, RET_DIM
    log_gamma = jnp.log1p(-(2.0 ** (-5.0 - jnp.arange(H, dtype=F32))))
    idx = jnp.arange(T, dtype=F32)
    chunk_of = jnp.arange(T) // CHUNK
    seen = chunk_of[None, :] <= chunk_of[:, None]
    dmat = jnp.where(seen[None], jnp.exp(log_gamma[:, None, None] * jnp.abs(idx[:, None] - idx[None, :])), 0.0)
    qdec = jnp.exp(log_gamma[:, None] * (idx + 1.0)[None, :])
    kdec = jnp.exp(log_gamma[:, None] * (T - 1 - idx)[None, :])
    sdec = jnp.exp(log_gamma * T)
    bc = lambda a: jnp.broadcast_to(a[:, :, None], (H, T, Dh))
    return dmat, bc(qdec), bc(kdec), jnp.broadcast_to(sdec[:, None, None], (H, 8, Dh))


def _split2(x):
    hi = x.astype(BF16)
    return hi, (x - hi.astype(F32)).astype(BF16)


def _split3(x):
    hi = x.astype(BF16)
    rest = x - hi.astype(F32)
    mid = rest.astype(BF16)
    return hi, mid, (rest - mid.astype(F32)).astype(BF16)


SB_STAGES = 5
SB_CARRY_STAGE = 4


def _sb_blocks(q_ref, k_ref, v_ref, acc_ref, carry_ref, j, diagonal, tq, exists=None):
    def visible(r, n):
        shape = (r.stop - r.start, n)
        return lax.broadcasted_iota(jnp.int32, shape, 1) < lax.broadcasted_iota(jnp.int32, shape, 0) + r.start

    def later(n):
        return jnp.where(lax.broadcasted_iota(jnp.int32, (n, n), 0) > lax.broadcasted_iota(jnp.int32, (n, n), 1),
                         1.0, 0.0).astype(BF16)

    start = pl.multiple_of(j * tq, tq)
    half = tq // 2
    row_panels = [(slice(0, half), half), (slice(half, tq), tq)] if diagonal else [(slice(0, tq), tq)]
    panels = [(r, n, slice(h * SB_DIM, (h + 1) * SB_DIM)) for h in range(SB_HEADS) for r, n in row_panels]
    heads_of = [h for h in range(SB_HEADS) for _ in row_panels]

    z = [lax.dot_general(q_ref[r, c], k_ref[pl.ds(start, n), c], (((1,), (1,)), ((), ())),
                         preferred_element_type=F32) for r, n, c in panels]
    yield
    log_beta = [jnp.minimum(x, 0.0) - jnp.log(1.0 + jnp.exp(-jnp.abs(x))) for x in z]
    log_keep = [lb - x for lb, x in zip(log_beta, z)]
    if diagonal:
        log_keep = [jnp.where(visible(r, n), lk, 0.0) for lk, (r, n, _) in zip(log_keep, panels)]
    if exists is not None:
        log_keep = [jnp.where(exists, lk, 0.0) for lk in log_keep]
    pieces = [_split2(lk) for lk in log_keep]
    yield
    tail_in = [jnp.dot(hi, later(n), preferred_element_type=F32) + jnp.dot(lo, later(n), preferred_element_type=F32)
               for (hi, lo), (_, n, _) in zip(pieces, panels)]
    yield
    total = [t[:, 0:1] + lk[:, 0:1] for t, lk in zip(tail_in, log_keep)]
    if diagonal:
        w = [jnp.where(visible(r, n), jnp.exp(lb + t), 0.0) for lb, t, (r, n, _) in zip(log_beta, tail_in, panels)]
    else:
        carry = [carry_ref[h, r, :] for h, (r, _, _) in zip(heads_of, panels)]
        w = [jnp.exp(lb + t + cy) for lb, t, cy in zip(log_beta, tail_in, carry)]
        total = [cy + tt for cy, tt in zip(carry, total)]
    if exists is not None:
        w = [jnp.where(exists, x, 0.0) for x in w]
    yield
    out = [jnp.dot(x.astype(BF16), v_ref[pl.ds(start, n), c], preferred_element_type=F32)
           for x, (_, n, c) in zip(w, panels)]
    for h, (r, _, c), tot, o in zip(heads_of, panels, total, out):
        carry_ref[h, r, :] = tot
        if diagonal:
            acc_ref[r, c] = o
        else:
            acc_ref[r, c] += o


def _sb_earlier_blocks(q_ref, k_ref, v_ref, acc_ref, carry_ref, i, tq):
    refs = (q_ref, k_ref, v_ref, acc_ref, carry_ref)

    def cond(s):
        j, top = s
        return jnp.logical_and(j >= 0, top > SB_DEAD_LOG_WEIGHT)

    def body(s):
        j, _ = s
        _run(_sb_blocks(*refs, j, False, tq))
        return j - 1, jnp.max(carry_ref[...])

    j_end, _ = lax.while_loop(cond, body, (i - 2, jnp.max(carry_ref[...])))
    return i - 2 - j_end


def _expand_heads(v):
    T = v.shape[0]
    lane = lax.broadcasted_iota(jnp.int32, (T, LANES), 1)
    tiles = []
    for j in range(SSM_W // LANES):
        a = jnp.broadcast_to(v[:, 2 * j:2 * j + 1], (T, LANES))
        b = jnp.broadcast_to(v[:, 2 * j + 1:2 * j + 2], (T, LANES))
        tiles.append(jnp.where(lane < SSM_HEAD_DIM, a, b))
    return jnp.concatenate(tiles, axis=1)


def _ssm_tile(xc_ref, z_ref, dt_ref, dtb_ref, alog_ref, dskip_ref, gain_ref, state_ref, wout_ref, T):
    G, Hg, P, N = SSM_GROUPS, SSM_HEADS // SSM_GROUPS, SSM_HEAD_DIM, SSM_STATE
    xs = xc_ref[:, :SSM_W].astype(F32)
    bm = [xc_ref[:, SSM_W + g * N:SSM_W + (g + 1) * N] for g in range(G)]
    cm = [xc_ref[:, SSM_W + (G + g) * N:SSM_W + (G + g + 1) * N] for g in range(G)]

    dt_in = dt_ref[...] + dtb_ref[...]
    dt = jnp.maximum(dt_in, 0.0) + jnp.log1p(jnp.exp(-jnp.abs(dt_in)))
    da = dt * (-jnp.exp(alog_ref[...]))
    row = lax.broadcasted_iota(jnp.int32, (T, T), 0)
    col = lax.broadcasted_iota(jnp.int32, (T, T), 1)
    causal = row >= col
    ones_below = jnp.where(causal, 1.0, 0.0).astype(BF16)
    acum = sum(jnp.dot(ones_below, piece, preferred_element_type=F32) for piece in _split3(da))
    group_cols = [slice(g * Hg * P, (g + 1) * Hg * P) for g in range(G)]
    half = T // 2
    top, bot = slice(0, half), slice(half, T)
    causal_top = (lax.broadcasted_iota(jnp.int32, (half, half), 0)
                  >= lax.broadcasted_iota(jnp.int32, (half, half), 1))
    causal_bot = (lax.broadcasted_iota(jnp.int32, (T - half, T), 0) + half
                  >= lax.broadcasted_iota(jnp.int32, (T - half, T), 1))
    nt = (((1,), (1,)), ((), ()))
    cb_top = [lax.dot_general(cm[g][top], bm[g][top], nt, preferred_element_type=F32) for g in range(G)]
    cb_bot = [lax.dot_general(cm[g][bot], bm[g], nt, preferred_element_type=F32) for g in range(G)]
    state = [state_ref[g] for g in range(G)]
    yield
    acum_t = acum.T
    last = acum[T - 1:T, :]
    dt_x = _expand_heads(dt)
    grow_x = _expand_heads(jnp.exp(acum))
    end_x = _expand_heads(jnp.exp(last - acum))
    xdt = xs * dt_x
    xdt_b = xdt.astype(BF16)
    xend_b = (xdt * end_x).astype(BF16)
    yield
    y_inter = [jnp.dot(cm[g], state[g].astype(BF16), preferred_element_type=F32) * grow_x[:, group_cols[g]]
               for g in range(G)]
    w_top, w_bot = [], []
    for h in range(SSM_HEADS):
        decay_top = jnp.where(causal_top, jnp.exp(acum[top, h:h + 1] - acum_t[h:h + 1, top]), 0.0)
        decay_bot = jnp.where(causal_bot, jnp.exp(acum[bot, h:h + 1] - acum_t[h:h + 1, :]), 0.0)
        w_top.append((cb_top[h // Hg] * decay_top).astype(BF16))
        w_bot.append((cb_bot[h // Hg] * decay_bot).astype(BF16))
        if h % 2 == 1:
            yield
    y_intra = [jnp.concatenate([
        jnp.dot(w_top[h], xdt_b[top, h * P:(h + 1) * P], preferred_element_type=F32),
        jnp.dot(w_bot[h], xdt_b[:, h * P:(h + 1) * P], preferred_element_type=F32)], axis=0)
        for h in range(SSM_HEADS)]
    for g in range(G):
        state_ref[g] = state[g] * grow_x[T - 1:T, group_cols[g]] + lax.dot_general(
            bm[g], xend_b[:, group_cols[g]], (((0,), (0,)), ((), ())), preferred_element_type=F32)
    yield
    y = jnp.concatenate(y_intra, axis=1) + jnp.concatenate(y_inter, axis=1) + xs * dskip_ref[...]
    y = y * z_ref[...].astype(F32)
    y = (y * lax.rsqrt(jnp.mean(y * y, axis=-1, keepdims=True) + NORM_EPS) * gain_ref[...]).astype(BF16)
    yield
    return jnp.dot(y, wout_ref[...], preferred_element_type=F32)


def _mixer_kernel(proj_ref, sk_ref, sv_ref, dt_ref,
                  h_ref, mod_ref, wout_ref, dmat_ref, qdec_ref, kdec_ref, sdec_ref, gn_ref,
                  dtb_ref, alog_ref, dskip_ref, sgain_ref,
                  o_ref, ret_state_ref, ssm_state_ref, sb_acc_ref, sb_carry_ref, mixed_ref, *, T):
    i = pl.program_id(1)
    group = lambda n: proj_ref.at[:, n * GROUP_W:(n + 1) * GROUP_W]
    rq_ref, rk_ref, rv_ref, rg_ref, sq_ref, mz_ref = group(0), group(1), group(2), group(3), group(4), group(7)
    mxc_ref = proj_ref.at[:, PROJ_W - SSM_XBC:PROJ_W]

    @pl.when(i == 0)
    def _():
        ret_state_ref[...] = jnp.zeros_like(ret_state_ref)
        ssm_state_ref[...] = jnp.zeros_like(ssm_state_ref)

    sb_refs = (sq_ref, sk_ref, sv_ref, sb_acc_ref, sb_carry_ref)
    _, mixed_ssm, mixed_ret, _ = _interleave(
        _sb_blocks(*sb_refs, i, True, T),
        _ssm_tile(mxc_ref, mz_ref, dt_ref, dtb_ref, alog_ref, dskip_ref, sgain_ref, ssm_state_ref,
                  wout_ref.at[RET_W + SB_W:MIX_W], T),
        _ret_tile(rq_ref, rk_ref, rv_ref, rg_ref, dmat_ref, qdec_ref, kdec_ref, sdec_ref, gn_ref, ret_state_ref,
                  wout_ref.at[0:RET_W]),
        _delayed(_sb_blocks(*sb_refs, jnp.maximum(i - 1, 0), False, T, exists=i > 0), SB_STAGES - SB_CARRY_STAGE + 1))
    mixed_ref[...] = mixed_ssm + mixed_ret

    def finish():
        mixed = mixed_ref[...] + jnp.dot(sb_acc_ref[...].astype(BF16), wout_ref[RET_W:RET_W + SB_W, :],
                                         preferred_element_type=F32)
        o_ref[...] = h_ref[...] + (1.0 + mod_ref[2:3, :]) * mixed

    finish()
    walked = _sb_earlier_blocks(*sb_refs, i, T)
    pl.when(walked > 0)(finish)


def _mixer(proj, dt_pad, h, mod3, layer, w_out, ret_consts, gn, dt_bias, a_log, d_skip, sgain):
    B, S, D = h.shape
    T = min(MIX_TILE, S)
    dmat, qdec, kdec, sdec = ret_consts
    whole_seq = lambda n: pl.BlockSpec((None, S, GROUP_W), lambda b, i: (b, 0, n), pipeline_mode=pl.Buffered(1))
    whole = lambda a: pl.BlockSpec(a.shape, lambda b, i: (0,) * a.ndim)
    pad_heads = lambda a: jnp.pad(a.reshape(1, SSM_HEADS), ((0, 0), (0, DT_PAD - SSM_HEADS)))
    small = (dmat, qdec, kdec, sdec, gn, pad_heads(dt_bias), pad_heads(a_log),
             jnp.repeat(d_skip, SSM_HEAD_DIM).reshape(1, SSM_W), sgain.reshape(1, SSM_W))
    return pl.pallas_call(
        functools.partial(_mixer_kernel, T=T),
        grid=(B, S // T),
        in_specs=[pl.BlockSpec((None, T, PROJ_W), lambda b, i: (b, i, 0)),
                  whole_seq(5), whole_seq(6),
                  pl.BlockSpec((None, T, DT_PAD), lambda b, i: (b, i, 0)),
                  pl.BlockSpec((None, T, D), lambda b, i: (b, i, 0)),
                  pl.BlockSpec((None, 3, D), lambda b, i: (b, 0, 0)),
                  pl.BlockSpec((None, MIX_W, D), lambda b, i: (layer, 0, 0))] + [whole(a) for a in small],
        out_specs=pl.BlockSpec((None, T, D), lambda b, i: (b, i, 0)),
        out_shape=jax.ShapeDtypeStruct((B, S, D), F32),
        scratch_shapes=[pltpu.VMEM((RET_HEADS, RET_DIM, RET_DIM), F32),
                        pltpu.VMEM((SSM_GROUPS, SSM_STATE, SSM_W // SSM_GROUPS), F32),
                        pltpu.VMEM((T, SB_W), F32),
                        pltpu.VMEM((SB_HEADS, T, 1), F32),
                        pltpu.VMEM((T, D), F32)],
        compiler_params=_cparams("arbitrary", "arbitrary"),
        name="mixer",
    )(proj, proj, proj, dt_pad, h, mod3, w_out, *small)


def kernel(x, c, ada_w, ada_b, norm_ffn1, ffn1_wg, ffn1_wu, ffn1_wd, norm_mix, w_in, conv_w, conv_b,
           dt_bias, a_log, d_skip, ret_gn, ssm_norm, w_out, norm_ffn2, ffn2_wg, ffn2_wu, ffn2_wd,
           final_ada_w, final_ada_b, final_norm):
    B, S, D = x.shape
    assert D == D_MODEL and S % CHUNK == 0 and B <= 8
    c_pad = jnp.pad(c, ((0, 8 - B), (0, 0)))
    mod = _ada_proj(c_pad, ada_w, ada_b[:, None, :], 3 * N_SUB * D // 4)[:, :B].reshape(DEPTH, B, 3 * N_SUB, D)
    fmod = _ada_proj(c_pad, final_ada_w[None], final_ada_b[None, None, :], 1024)[0, :B].reshape(B, 2, D)
    cos2, sin2 = _rope_tables(S)
    ret_consts = _retention_consts(S)
    ffn1_w = [_to_bf16(w) for w in (ffn1_wg, ffn1_wu, ffn1_wd)]
    ffn2_w = [_to_bf16(w) for w in (ffn2_wg, ffn2_wu, ffn2_wd)]
    w_main = w_in.astype(BF16)
    w_dt = jnp.pad(w_main[:, :, PROJ_W:], ((0, 0), (0, 0), (0, DT_PAD - SSM_HEADS)))
    w_out_b = _to_bf16(w_out)

    h = x
    for l in range(DEPTH):
        h = _ffn(h, mod[l, :, 0:3], norm_ffn1[l][None], l, *ffn1_w)
        proj, dt_pad = _inproj(h, mod[l, :, 3:6], norm_mix[l][None], l, w_main, w_dt, cos2, sin2,
                               conv_w[l], conv_b[l])
        h = _mixer(proj, dt_pad, h, mod[l, :, 3:6], l, w_out_b, ret_consts,
                   ret_gn[l].reshape(RET_HEADS, 1, RET_DIM), dt_bias[l], a_log[l], d_skip[l], ssm_norm[l])
        last = l == DEPTH - 1
        h = _ffn(h, mod[l, :, 6:9], norm_ffn2[l][None], l, *ffn2_w,
                 final_mod=fmod if last else None, final_gain=final_norm[None] if last else None)
    return h
```

```python
import functools

import jax
import jax.numpy as jnp
from jax import lax
from jax.experimental import pallas as pl
from jax.experimental.pallas import tpu as pltpu

D_MODEL = 1024
DEPTH = 2
CHUNK = 64
RET_HEADS = 4
RET_DIM = 128
SB_HEADS = 4
SB_DIM = 128
SSM_HEADS = 8
SSM_HEAD_DIM = 64
SSM_STATE = 128
SSM_GROUPS = 2
SSM_CONV = 4
D_FF = 2816
ROPE_BASE = 10000.0
NORM_EPS = 1e-6
N_SUB = 3

RET_W = RET_HEADS * RET_DIM
SB_W = SB_HEADS * SB_DIM
SSM_W = SSM_HEADS * SSM_HEAD_DIM
MIX_W = RET_W + SB_W + SSM_W
SSM_XBC = SSM_W + 2 * SSM_GROUPS * SSM_STATE
PROJ_W = 4 * RET_W + 3 * SB_W + SSM_W + SSM_XBC
GROUP_W = 512
LANES = 128
DT_PAD = LANES
CONV_PAD = 8

F32 = jnp.float32
BF16 = jnp.bfloat16

TOKEN_TILE = 1024
FFN_TILE = 1024
FF_CHUNK = 256
CAST_ROWS = 256
MIX_TILE = 256
SB_DEAD_LOG_WEIGHT = -104.0
VMEM_LIMIT = 56 * 1024 * 1024
MIXER_VMEM_LIMIT = 62 * 1024 * 1024


def _cparams(*sem, vmem_limit=VMEM_LIMIT):
    return pltpu.CompilerParams(dimension_semantics=sem, vmem_limit_bytes=vmem_limit)


def _silu(x):
    return x / (1.0 + jnp.exp(-x))


def _norm_mod(h, gain, shift, scale):
    y = h * lax.rsqrt(jnp.mean(h * h, axis=-1, keepdims=True) + NORM_EPS) * gain
    return y * (1.0 + scale) + shift


def _ada_kernel(c_ref, w_ref, b_ref, o_ref):
    cond = _silu(c_ref[...])
    o_ref[...] = jnp.dot(cond, w_ref[...], precision=lax.Precision.HIGHEST,
                         preferred_element_type=F32) + b_ref[...]


def _ada_proj(c_pad, w, b, tn):
    L, D, N = w.shape
    return pl.pallas_call(
        _ada_kernel,
        grid=(L, N // tn),
        in_specs=[pl.BlockSpec((8, D), lambda l, j: (0, 0)),
                  pl.BlockSpec((None, D, tn), lambda l, j: (l, 0, j)),
                  pl.BlockSpec((None, 1, tn), lambda l, j: (l, 0, j))],
        out_specs=pl.BlockSpec((None, 8, tn), lambda l, j: (l, 0, j)),
        out_shape=jax.ShapeDtypeStruct((L, 8, N), F32),
        compiler_params=_cparams("arbitrary", "arbitrary"),
        name="ada_proj",
    )(c_pad, w, b)


def _cast_kernel(w_ref, o_ref):
    o_ref[...] = w_ref[...].astype(o_ref.dtype)


def _to_bf16(w, cols=None):
    L, R, C = w.shape
    cols = C if cols is None else cols
    tr = CAST_ROWS
    assert R % tr == 0 and (cols == C or cols % LANES == 0)
    return pl.pallas_call(
        _cast_kernel,
        grid=(L, R // tr),
        in_specs=[pl.BlockSpec((None, tr, cols), lambda l, r: (l, r, 0))],
        out_specs=pl.BlockSpec((None, tr, cols), lambda l, r: (l, r, 0)),
        out_shape=jax.ShapeDtypeStruct((L, R, cols), BF16),
        compiler_params=_cparams("arbitrary", "arbitrary"),
        name="to_bf16",
    )(w)


def _ffn_kernel(h_ref, mod_ref, gain_ref, wg_ref, wu_ref, wd_ref, *rest, final):
    o_ref = rest[-1]
    h = h_ref[...]
    u = _norm_mod(h, gain_ref[...], mod_ref[0:1, :], mod_ref[1:2, :]).astype(BF16)
    acc = jnp.zeros(h.shape, F32)
    for j in range(D_FF // FF_CHUNK):
        cols = slice(j * FF_CHUNK, (j + 1) * FF_CHUNK)
        g = jnp.dot(u, wg_ref[:, cols], preferred_element_type=F32)
        up = jnp.dot(u, wu_ref[:, cols], preferred_element_type=F32)
        a = (_silu(g) * up).astype(BF16)
        acc = acc + jnp.dot(a, wd_ref[cols, :], preferred_element_type=F32)
    out = h + (0.5 * (1.0 + mod_ref[2:3, :])) * acc
    if final:
        fmod_ref, fgain_ref = rest[0], rest[1]
        out = _norm_mod(out, fgain_ref[...], fmod_ref[0:1, :], fmod_ref[1:2, :])
    o_ref[...] = out


def _ffn(h, mod3, gain, layer, wg, wu, wd, final_mod=None, final_gain=None):
    B, S, D = h.shape
    tm = min(FFN_TILE, S)
    const = lambda b, i: (0, 0)
    of_layer = lambda b, i: (layer, 0, 0)
    final = final_mod is not None
    extra_specs = [pl.BlockSpec((None, 2, D), lambda b, i: (b, 0, 0)), pl.BlockSpec((1, D), const)] if final else []
    extra = (final_mod, final_gain) if final else ()
    return pl.pallas_call(
        functools.partial(_ffn_kernel, final=final),
        grid=(B, S // tm),
        in_specs=[pl.BlockSpec((None, tm, D), lambda b, i: (b, i, 0)),
                  pl.BlockSpec((None, 3, D), lambda b, i: (b, 0, 0)),
                  pl.BlockSpec((1, D), const),
                  pl.BlockSpec((None, D, D_FF), of_layer),
                  pl.BlockSpec((None, D, D_FF), of_layer),
                  pl.BlockSpec((None, D_FF, D), of_layer)] + extra_specs,
        out_specs=pl.BlockSpec((None, tm, D), lambda b, i: (b, i, 0)),
        out_shape=jax.ShapeDtypeStruct((B, S, D), F32),
        compiler_params=_cparams("arbitrary", "arbitrary"),
        name="ffn",
    )(h, mod3, gain, wg, wu, wd, *extra)


def _inproj_kernel(h_ref, mod_ref, gain_ref, w_ref, wdt_ref, cos_ref, sin_ref, convw_ref, convb_ref,
                   proj_ref, dt_ref, xbuf_ref):
    tm = h_ref.shape[0]

    @pl.when(pl.program_id(1) == 0)
    def _():
        xbuf_ref[0:CONV_PAD, :] = jnp.zeros((CONV_PAD, SSM_XBC), F32)

    u = _norm_mod(h_ref[...], gain_ref[...], mod_ref[0:1, :], mod_ref[1:2, :]).astype(BF16)
    cos2 = cos_ref[...]
    sin2 = sin_ref[...]

    def rope(x):
        heads = [x[:, a * RET_DIM:(a + 1) * RET_DIM] for a in range(x.shape[1] // RET_DIM)]
        return jnp.concatenate([y * cos2 + pltpu.roll(y, RET_DIM // 2, 1) * sin2 for y in heads], axis=1)

    post = {0: rope,
            1: lambda y: rope(y) * (RET_DIM ** -0.5),
            3: _silu,
            4: lambda y: y * (SB_DIM ** -0.5),
            7: _silu}
    def project(j):
        return jnp.dot(u, w_ref[:, j * GROUP_W:(j + 1) * GROUP_W], preferred_element_type=F32)

    xbc0 = (PROJ_W - SSM_XBC) // GROUP_W
    for j in range(SSM_XBC // GROUP_W):
        xbuf_ref[CONV_PAD:CONV_PAD + tm, j * GROUP_W:(j + 1) * GROUP_W] = project(xbc0 + j)
    dt_ref[...] = jnp.dot(u, wdt_ref[...], preferred_element_type=F32)
    for j in range(SSM_XBC // GROUP_W):
        cols = slice(j * GROUP_W, (j + 1) * GROUP_W)
        rows = xbuf_ref[:, cols]
        conv = convb_ref[:, cols] + convw_ref[SSM_CONV - 1:SSM_CONV, cols] * rows[CONV_PAD:]
        for back in range(1, SSM_CONV):
            tap = convw_ref[SSM_CONV - 1 - back:SSM_CONV - back, cols]
            conv = conv + tap * pltpu.roll(rows, back, 0)[CONV_PAD:]
        proj_ref[:, (xbc0 + j) * GROUP_W:(xbc0 + j + 1) * GROUP_W] = _silu(conv).astype(proj_ref.dtype)
    xbuf_ref[0:CONV_PAD, :] = xbuf_ref[tm:tm + CONV_PAD, :]
    for j in range(xbc0):
        cols = slice(j * GROUP_W, (j + 1) * GROUP_W)
        proj_ref[:, cols] = post.get(j, lambda y: y)(project(j)).astype(proj_ref.dtype)


def _inproj(h, mod3, gain, layer, w_main, w_dt, cos2, sin2, conv_w, conv_b):
    B, S, D = h.shape
    tm = min(TOKEN_TILE, S)
    const = lambda b, i: (0, 0)
    of_layer = lambda b, i: (layer, 0, 0)
    return pl.pallas_call(
        _inproj_kernel,
        grid=(B, S // tm),
        in_specs=[pl.BlockSpec((None, tm, D), lambda b, i: (b, i, 0)),
                  pl.BlockSpec((None, 3, D), lambda b, i: (b, 0, 0)),
                  pl.BlockSpec((1, D), const),
                  pl.BlockSpec((None, D, PROJ_W), of_layer),
                  pl.BlockSpec((None, D, DT_PAD), of_layer),
                  pl.BlockSpec((tm, RET_DIM), lambda b, i: (i, 0)),
                  pl.BlockSpec((tm, RET_DIM), lambda b, i: (i, 0)),
                  pl.BlockSpec((SSM_CONV, SSM_XBC), const),
                  pl.BlockSpec((1, SSM_XBC), const)],
        out_specs=[pl.BlockSpec((None, tm, PROJ_W), lambda b, i: (b, i, 0)),
                   pl.BlockSpec((None, tm, DT_PAD), lambda b, i: (b, i, 0))],
        out_shape=[jax.ShapeDtypeStruct((B, S, PROJ_W), BF16),
                   jax.ShapeDtypeStruct((B, S, DT_PAD), F32)],
        scratch_shapes=[pltpu.VMEM((tm + CONV_PAD, SSM_XBC), F32)],
        compiler_params=_cparams("arbitrary", "arbitrary"),
        name="inproj",
    )(h, mod3, gain, w_main, w_dt, cos2, sin2, conv_w, conv_b.reshape(1, SSM_XBC))


def _rope_tables(S):
    half = RET_DIM // 2
    pos = jnp.arange(S, dtype=F32)
    inv_freq = ROPE_BASE ** (-jnp.arange(half, dtype=F32) / half)
    ang = pos[:, None] * inv_freq[None, :]
    cos, sin = jnp.cos(ang), jnp.sin(ang)
    return jnp.concatenate([cos, cos], axis=-1), jnp.concatenate([-sin, sin], axis=-1)


def _run(stages):
    return _interleave(stages)[0]


def _interleave(*staged):
    values = [None] * len(staged)
    live = list(enumerate(staged))
    while live:
        still = []
        for n, stages in live:
            try:
                next(stages)
                still.append((n, stages))
            except StopIteration as done:
                values[n] = done.value
        live = still
    return values


def _delayed(stages, rounds):
    for _ in range(rounds):
        yield
    return (yield from stages)


def _ret_tile(q_ref, k_ref, v_ref, g_ref, dmat_ref, qdec_ref, kdec_ref, sdec_ref, gn_ref, state_ref, wout_ref):
    heads = range(RET_HEADS)
    cols = [slice(h * RET_DIM, (h + 1) * RET_DIM) for h in heads]
    nt = (((1,), (1,)), ((), ()))
    tn = (((0,), (0,)), ((), ()))
    state = [state_ref[h] for h in heads]
    scores = [lax.dot_general(q_ref[:, c], k_ref[:, c], nt, preferred_element_type=F32) for c in cols]
    cross = [jnp.dot(q_ref[:, cols[h]], state[h].astype(BF16), preferred_element_type=F32) for h in heads]
    yield
    vdec = [(kdec_ref[h] * v_ref[:, cols[h]].astype(F32)).astype(BF16) for h in heads]
    kv = [lax.dot_general(k_ref[:, cols[h]], vdec[h], tn, preferred_element_type=F32) for h in heads]
    yield
    weights = [(scores[h] * dmat_ref[h]).astype(BF16) for h in heads]
    yield
    y = [jnp.dot(weights[h], v_ref[:, cols[h]], preferred_element_type=F32) + qdec_ref[h] * cross[h] for h in heads]
    yield
    outs = []
    for h in heads:
        state_ref[h] = state[h] * sdec_ref[h, 0:1, :] + kv[h]
        yn = y[h] * lax.rsqrt(jnp.mean(y[h] * y[h], axis=-1, keepdims=True) + NORM_EPS) * gn_ref[h]
        outs.append((yn * g_ref[:, cols[h]].astype(F32)).astype(BF16))
    yield
    return sum(jnp.dot(outs[h], wout_ref[cols[h], :], preferred_element_type=F32) for h in heads)


def _retention_consts(S):
    T = min(MIX_TILE, S)
    H, Dh = RET_HEADS, RET_DIM
    log_gamma = jnp.log1p(-(2.0 ** (-5.0 - jnp.arange(H, dtype=F32))))
    idx = jnp.arange(T, dtype=F32)
    chunk_of = jnp.arange(T) // CHUNK
    seen = chunk_of[None, :] <= chunk_of[:, None]
    dmat = jnp.where(seen[None], jnp.exp(log_gamma[:, None, None] * jnp.abs(idx[:, None] - idx[None, :])), 0.0)
    qdec = jnp.exp(log_gamma[:, None] * (idx + 1.0)[None, :])
    kdec = jnp.exp(log_gamma[:, None] * (T - 1 - idx)[None, :])
    sdec = jnp.exp(log_gamma * T)
    bc = lambda a: jnp.broadcast_to(a[:, :, None], (H, T, Dh))
    return dmat, bc(qdec), bc(kdec), jnp.broadcast_to(sdec[:, None, None], (H, 8, Dh))


def _split2(x):
    hi = x.astype(BF16)
    return hi, (x - hi.astype(F32)).astype(BF16)


def _split3(x):
    hi = x.astype(BF16)
    rest = x - hi.astype(F32)
    mid = rest.astype(BF16)
    return hi, mid, (rest - mid.astype(F32)).astype(BF16)


SB_STAGES = 5
SB_CARRY_STAGE = 4


def _sb_blocks(q_ref, k_ref, v_ref, acc_ref, carry_ref, j, diagonal, tq, exists=None):
    def visible(r, n):
        shape = (r.stop - r.start, n)
        return lax.broadcasted_iota(jnp.int32, shape, 1) < lax.broadcasted_iota(jnp.int32, shape, 0) + r.start

    def later(n):
        return jnp.where(lax.broadcasted_iota(jnp.int32, (n, n), 0) > lax.broadcasted_iota(jnp.int32, (n, n), 1),
                         1.0, 0.0).astype(BF16)

    start = pl.multiple_of(j * tq, tq)
    half = tq // 2
    row_panels = [(slice(0, half), half), (slice(half, tq), tq)] if diagonal else [(slice(0, tq), tq)]
    panels = [(r, n, slice(h * SB_DIM, (h + 1) * SB_DIM)) for h in range(SB_HEADS) for r, n in row_panels]
    heads_of = [h for h in range(SB_HEADS) for _ in row_panels]

    z = [lax.dot_general(q_ref[r, c], k_ref[pl.ds(start, n), c], (((1,), (1,)), ((), ())),
                         preferred_element_type=F32) for r, n, c in panels]
    yield
    log_beta = [jnp.minimum(x, 0.0) - jnp.log(1.0 + jnp.exp(-jnp.abs(x))) for x in z]
    log_keep = [lb - x for lb, x in zip(log_beta, z)]
    if diagonal:
        log_keep = [jnp.where(visible(r, n), lk, 0.0) for lk, (r, n, _) in zip(log_keep, panels)]
    if exists is not None:
        log_keep = [jnp.where(exists, lk, 0.0) for lk in log_keep]
    pieces = [_split2(lk) for lk in log_keep]
    yield
    tail_in = [jnp.dot(hi, later(n), preferred_element_type=F32) + jnp.dot(lo, later(n), preferred_element_type=F32)
               for (hi, lo), (_, n, _) in zip(pieces, panels)]
    yield
    total = [t[:, 0:1] + lk[:, 0:1] for t, lk in zip(tail_in, log_keep)]
    if diagonal:
        w = [jnp.where(visible(r, n), jnp.exp(lb + t), 0.0) for lb, t, (r, n, _) in zip(log_beta, tail_in, panels)]
    else:
        carry = [carry_ref[h, r, :] for h, (r, _, _) in zip(heads_of, panels)]
        w = [jnp.exp(lb + t + cy) for lb, t, cy in zip(log_beta, tail_in, carry)]
        total = [cy + tt for cy, tt in zip(carry, total)]
    if exists is not None:
        w = [jnp.where(exists, x, 0.0) for x in w]
    yield
    out = [jnp.dot(x.astype(BF16), v_ref[pl.ds(start, n), c], preferred_element_type=F32)
           for x, (_, n, c) in zip(w, panels)]
    for h, (r, _, c), tot, o in zip(heads_of, panels, total, out):
        carry_ref[h, r, :] = tot
        if diagonal:
            acc_ref[r, c] = o
        else:
            acc_ref[r, c] += o


def _sb_earlier_blocks(q_ref, k_ref, v_ref, acc_ref, carry_ref, i, tq):
    refs = (q_ref, k_ref, v_ref, acc_ref, carry_ref)

    def cond(s):
        j, top = s
        return jnp.logical_and(j >= 0, top > SB_DEAD_LOG_WEIGHT)

    def body(s):
        j, _ = s
        _run(_sb_blocks(*refs, j, False, tq))
        return j - 1, jnp.max(carry_ref[...])

    j_end, _ = lax.while_loop(cond, body, (i - 2, jnp.max(carry_ref[...])))
    return i - 2 - j_end


def _expand_heads(v):
    T = v.shape[0]
    lane = lax.broadcasted_iota(jnp.int32, (T, LANES), 1)
    tiles = []
    for j in range(SSM_W // LANES):
        a = jnp.broadcast_to(v[:, 2 * j:2 * j + 1], (T, LANES))
        b = jnp.broadcast_to(v[:, 2 * j + 1:2 * j + 2], (T, LANES))
        tiles.append(jnp.where(lane < SSM_HEAD_DIM, a, b))
    return jnp.concatenate(tiles, axis=1)


def _ssm_tile(xc_ref, z_ref, dt_ref, dtb_ref, alog_ref, dskip_ref, gain_ref, state_ref, wout_ref, T):
    G, Hg, P, N = SSM_GROUPS, SSM_HEADS // SSM_GROUPS, SSM_HEAD_DIM, SSM_STATE
    xs = xc_ref[:, :SSM_W].astype(F32)
    bm = [xc_ref[:, SSM_W + g * N:SSM_W + (g + 1) * N] for g in range(G)]
    cm = [xc_ref[:, SSM_W + (G + g) * N:SSM_W + (G + g + 1) * N] for g in range(G)]

    dt_in = dt_ref[...] + dtb_ref[...]
    dt = jnp.maximum(dt_in, 0.0) + jnp.log1p(jnp.exp(-jnp.abs(dt_in)))
    da = dt * (-jnp.exp(alog_ref[...]))
    row = lax.broadcasted_iota(jnp.int32, (T, T), 0)
    col = lax.broadcasted_iota(jnp.int32, (T, T), 1)
    causal = row >= col
    ones_below = jnp.where(causal, 1.0, 0.0).astype(BF16)
    acum = sum(jnp.dot(ones_below, piece, preferred_element_type=F32) for piece in _split3(da))
    group_cols = [slice(g * Hg * P, (g + 1) * Hg * P) for g in range(G)]
    half = T // 2
    top, bot = slice(0, half), slice(half, T)
    causal_top = (lax.broadcasted_iota(jnp.int32, (half, half), 0)
                  >= lax.broadcasted_iota(jnp.int32, (half, half), 1))
    causal_bot = (lax.broadcasted_iota(jnp.int32, (T - half, T), 0) + half
                  >= lax.broadcasted_iota(jnp.int32, (T - half, T), 1))
    nt = (((1,), (1,)), ((), ()))
    cb_top = [lax.dot_general(cm[g][top], bm[g][top], nt, preferred_element_type=F32) for g in range(G)]
    cb_bot = [lax.dot_general(cm[g][bot], bm[g], nt, preferred_element_type=F32) for g in range(G)]
    state = [state_ref[g] for g in range(G)]
    yield
    acum_t = acum.T
    last = acum[T - 1:T, :]
    dt_x = _expand_heads(dt)
    grow_x = _expand_heads(jnp.exp(acum))
    end_x = _expand_heads(jnp.exp(last - acum))
    xdt = xs * dt_x
    xdt_b = xdt.astype(BF16)
    xend_b = (xdt * end_x).astype(BF16)
    yield
    y_inter = [jnp.dot(cm[g], state[g].astype(BF16), preferred_element_type=F32) * grow_x[:, group_cols[g]]
               for g in range(G)]
    w_top, w_bot = [], []
    for h in range(SSM_HEADS):
        decay_top = jnp.where(causal_top, jnp.exp(acum[top, h:h + 1] - acum_t[h:h + 1, top]), 0.0)
        decay_bot = jnp.where(causal_bot, jnp.exp(acum[bot, h:h + 1] - acum_t[h:h + 1, :]), 0.0)
        w_top.append((cb_top[h // Hg] * decay_top).astype(BF16))
        w_bot.append((cb_bot[h // Hg] * decay_bot).astype(BF16))
        if h % 2 == 1:
            yield
    y_intra = [jnp.concatenate([
        jnp.dot(w_top[h], xdt_b[top, h * P:(h + 1) * P], preferred_element_type=F32),
        jnp.dot(w_bot[h], xdt_b[:, h * P:(h + 1) * P], preferred_element_type=F32)], axis=0)
        for h in range(SSM_HEADS)]
    for g in range(G):
        state_ref[g] = state[g] * grow_x[T - 1:T, group_cols[g]] + lax.dot_general(
            bm[g], xend_b[:, group_cols[g]], (((0,), (0,)), ((), ())), preferred_element_type=F32)
    yield
    y = jnp.concatenate(y_intra, axis=1) + jnp.concatenate(y_inter, axis=1) + xs * dskip_ref[...]
    y = y * z_ref[...].astype(F32)
    y = (y * lax.rsqrt(jnp.mean(y * y, axis=-1, keepdims=True) + NORM_EPS) * gain_ref[...]).astype(BF16)
    yield
    return jnp.dot(y, wout_ref[...], preferred_element_type=F32)


def _mixer_kernel(proj_ref, sk_ref, sv_ref, dt_ref,
                  h_ref, mod_ref, wout_ref, dmat_ref, qdec_ref, kdec_ref, sdec_ref, gn_ref,
                  dtb_ref, alog_ref, dskip_ref, sgain_ref,
                  o_ref, ret_state_ref, ssm_state_ref, sb_acc_ref, sb_carry_ref, mixed_ref, *, T):
    i = pl.program_id(1)
    group = lambda n: proj_ref.at[:, n * GROUP_W:(n + 1) * GROUP_W]
    rq_ref, rk_ref, rv_ref, rg_ref, sq_ref, mz_ref = group(0), group(1), group(2), group(3), group(4), group(7)
    mxc_ref = proj_ref.at[:, PROJ_W - SSM_XBC:PROJ_W]

    @pl.when(i == 0)
    def _():
        ret_state_ref[...] = jnp.zeros_like(ret_state_ref)
        ssm_state_ref[...] = jnp.zeros_like(ssm_state_ref)

    sb_refs = (sq_ref, sk_ref, sv_ref, sb_acc_ref, sb_carry_ref)
    _, mixed_ssm, mixed_ret, _ = _interleave(
        _sb_blocks(*sb_refs, i, True, T),
        _ssm_tile(mxc_ref, mz_ref, dt_ref, dtb_ref, alog_ref, dskip_ref, sgain_ref, ssm_state_ref,
                  wout_ref.at[RET_W + SB_W:MIX_W], T),
        _ret_tile(rq_ref, rk_ref, rv_ref, rg_ref, dmat_ref, qdec_ref, kdec_ref, sdec_ref, gn_ref, ret_state_ref,
                  wout_ref.at[0:RET_W]),
        _delayed(_sb_blocks(*sb_refs, jnp.maximum(i - 1, 0), False, T, exists=i > 0), SB_STAGES - SB_CARRY_STAGE + 1))
    mixed_ref[...] = mixed_ssm + mixed_ret

    def finish():
        mixed = mixed_ref[...] + jnp.dot(sb_acc_ref[...].astype(BF16), wout_ref[RET_W:RET_W + SB_W, :],
                                         preferred_element_type=F32)
        o_ref[...] = h_ref[...] + (1.0 + mod_ref[2:3, :]) * mixed

    finish()
    walked = _sb_earlier_blocks(*sb_refs, i, T)
    pl.when(walked > 0)(finish)


def _mixer(proj, dt_pad, h, mod3, layer, w_out, ret_consts, gn, dt_bias, a_log, d_skip, sgain):
    B, S, D = h.shape
    T = min(MIX_TILE, S)
    dmat, qdec, kdec, sdec = ret_consts
    whole_seq = lambda n: pl.BlockSpec((None, S, GROUP_W), lambda b, i: (b, 0, n))
    whole = lambda a: pl.BlockSpec(a.shape, lambda b, i: (0,) * a.ndim)
    pad_heads = lambda a: jnp.pad(a.reshape(1, SSM_HEADS), ((0, 0), (0, DT_PAD - SSM_HEADS)))
    small = (dmat, qdec, kdec, sdec, gn, pad_heads(dt_bias), pad_heads(a_log),
             jnp.repeat(d_skip, SSM_HEAD_DIM).reshape(1, SSM_W), sgain.reshape(1, SSM_W))
    return pl.pallas_call(
        functools.partial(_mixer_kernel, T=T),
        grid=(B, S // T),
        in_specs=[pl.BlockSpec((None, T, PROJ_W), lambda b, i: (b, i, 0)),
                  whole_seq(5), whole_seq(6),
                  pl.BlockSpec((None, T, DT_PAD), lambda b, i: (b, i, 0)),
                  pl.BlockSpec((None, T, D), lambda b, i: (b, i, 0)),
                  pl.BlockSpec((None, 3, D), lambda b, i: (b, 0, 0)),
                  pl.BlockSpec((None, MIX_W, D), lambda b, i: (layer, 0, 0))] + [whole(a) for a in small],
        out_specs=pl.BlockSpec((None, T, D), lambda b, i: (b, i, 0)),
        out_shape=jax.ShapeDtypeStruct((B, S, D), F32),
        scratch_shapes=[pltpu.VMEM((RET_HEADS, RET_DIM, RET_DIM), F32),
                        pltpu.VMEM((SSM_GROUPS, SSM_STATE, SSM_W // SSM_GROUPS), F32),
                        pltpu.VMEM((T, SB_W), F32),
                        pltpu.VMEM((SB_HEADS, T, 1), F32),
                        pltpu.VMEM((T, D), F32)],
        compiler_params=_cparams("arbitrary", "arbitrary", vmem_limit=MIXER_VMEM_LIMIT),
        name="mixer",
    )(proj, proj, proj, dt_pad, h, mod3, w_out, *small)


def kernel(x, c, ada_w, ada_b, norm_ffn1, ffn1_wg, ffn1_wu, ffn1_wd, norm_mix, w_in, conv_w, conv_b,
           dt_bias, a_log, d_skip, ret_gn, ssm_norm, w_out, norm_ffn2, ffn2_wg, ffn2_wu, ffn2_wd,
           final_ada_w, final_ada_b, final_norm):
    B, S, D = x.shape
    assert D == D_MODEL and S % CHUNK == 0 and B <= 8
    c_pad = jnp.pad(c, ((0, 8 - B), (0, 0)))
    mod = _ada_proj(c_pad, ada_w, ada_b[:, None, :], 3 * N_SUB * D // 4)[:, :B].reshape(DEPTH, B, 3 * N_SUB, D)
    fmod = _ada_proj(c_pad, final_ada_w[None], final_ada_b[None, None, :], 1024)[0, :B].reshape(B, 2, D)
    cos2, sin2 = _rope_tables(S)
    ret_consts = _retention_consts(S)
    ffn1_w = [_to_bf16(w) for w in (ffn1_wg, ffn1_wu, ffn1_wd)]
    ffn2_w = [_to_bf16(w) for w in (ffn2_wg, ffn2_wu, ffn2_wd)]
    w_main = w_in.astype(BF16)
    w_dt = jnp.pad(w_main[:, :, PROJ_W:], ((0, 0), (0, 0), (0, DT_PAD - SSM_HEADS)))
    w_out_b = _to_bf16(w_out)

    h = x
    for l in range(DEPTH):
        h = _ffn(h, mod[l, :, 0:3], norm_ffn1[l][None], l, *ffn1_w)
        proj, dt_pad = _inproj(h, mod[l, :, 3:6], norm_mix[l][None], l, w_main, w_dt, cos2, sin2,
                               conv_w[l], conv_b[l])
        h = _mixer(proj, dt_pad, h, mod[l, :, 3:6], l, w_out_b, ret_consts,
                   ret_gn[l].reshape(RET_HEADS, 1, RET_DIM), dt_bias[l], a_log[l], d_skip[l], ssm_norm[l])
        last = l == DEPTH - 1
        h = _ffn(h, mod[l, :, 6:9], norm_ffn2[l][None], l, *ffn2_w,
                 final_mod=fmod if last else None, final_gain=final_norm[None] if last else None)
    return h
```

```python
import functools

import jax
import jax.numpy as jnp
from jax import lax
from jax.experimental import pallas as pl
from jax.experimental.pallas import tpu as pltpu

D_MODEL = 1024
DEPTH = 2
CHUNK = 64
RET_HEADS = 4
RET_DIM = 128
SB_HEADS = 4
SB_DIM = 128
SSM_HEADS = 8
SSM_HEAD_DIM = 64
SSM_STATE = 128
SSM_GROUPS = 2
SSM_CONV = 4
D_FF = 2816
ROPE_BASE = 10000.0
NORM_EPS = 1e-6
N_SUB = 3

RET_W = RET_HEADS * RET_DIM
SB_W = SB_HEADS * SB_DIM
SSM_W = SSM_HEADS * SSM_HEAD_DIM
MIX_W = RET_W + SB_W + SSM_W
SSM_XBC = SSM_W + 2 * SSM_GROUPS * SSM_STATE
PROJ_W = 4 * RET_W + 3 * SB_W + SSM_W + SSM_XBC
GROUP_W = 512
LANES = 128
DT_PAD = LANES
CONV_PAD = 8

F32 = jnp.float32
BF16 = jnp.bfloat16

TOKEN_TILE = 1024
FFN_TILE = 1024
FF_CHUNK = 256
CAST_ROWS = 256
MIX_TILE = 256
SB_DEAD_LOG_WEIGHT = -104.0
VMEM_LIMIT = 56 * 1024 * 1024
MIXER_VMEM_LIMIT = 62 * 1024 * 1024


def _cparams(*sem, vmem_limit=VMEM_LIMIT):
    return pltpu.CompilerParams(dimension_semantics=sem, vmem_limit_bytes=vmem_limit)


def _silu(x):
    return x / (1.0 + jnp.exp(-x))


def _norm_mod(h, gain, shift, scale):
    y = h * lax.rsqrt(jnp.mean(h * h, axis=-1, keepdims=True) + NORM_EPS) * gain
    return y * (1.0 + scale) + shift


def _ada_kernel(c_ref, w_ref, b_ref, o_ref):
    cond = _silu(c_ref[...])
    o_ref[...] = jnp.dot(cond, w_ref[...], precision=lax.Precision.HIGHEST,
                         preferred_element_type=F32) + b_ref[...]


def _ada_proj(c_pad, w, b, tn):
    L, D, N = w.shape
    return pl.pallas_call(
        _ada_kernel,
        grid=(L, N // tn),
        in_specs=[pl.BlockSpec((8, D), lambda l, j: (0, 0)),
                  pl.BlockSpec((None, D, tn), lambda l, j: (l, 0, j)),
                  pl.BlockSpec((None, 1, tn), lambda l, j: (l, 0, j))],
        out_specs=pl.BlockSpec((None, 8, tn), lambda l, j: (l, 0, j)),
        out_shape=jax.ShapeDtypeStruct((L, 8, N), F32),
        compiler_params=_cparams("arbitrary", "arbitrary"),
        name="ada_proj",
    )(c_pad, w, b)


def _cast_kernel(w_ref, o_ref):
    o_ref[...] = w_ref[...].astype(o_ref.dtype)


def _to_bf16(w, cols=None):
    L, R, C = w.shape
    cols = C if cols is None else cols
    tr = CAST_ROWS
    assert R % tr == 0 and (cols == C or cols % LANES == 0)
    return pl.pallas_call(
        _cast_kernel,
        grid=(L, R // tr),
        in_specs=[pl.BlockSpec((None, tr, cols), lambda l, r: (l, r, 0))],
        out_specs=pl.BlockSpec((None, tr, cols), lambda l, r: (l, r, 0)),
        out_shape=jax.ShapeDtypeStruct((L, R, cols), BF16),
        compiler_params=_cparams("arbitrary", "arbitrary"),
        name="to_bf16",
    )(w)


def _ffn_kernel(h_ref, mod_ref, gain_ref, wg_ref, wu_ref, wd_ref, *rest, final):
    o_ref = rest[-1]
    h = h_ref[...]
    u = _norm_mod(h, gain_ref[...], mod_ref[0:1, :], mod_ref[1:2, :]).astype(BF16)
    acc = jnp.zeros(h.shape, F32)
    for j in range(D_FF // FF_CHUNK):
        cols = slice(j * FF_CHUNK, (j + 1) * FF_CHUNK)
        g = jnp.dot(u, wg_ref[:, cols], preferred_element_type=F32)
        up = jnp.dot(u, wu_ref[:, cols], preferred_element_type=F32)
        a = (_silu(g) * up).astype(BF16)
        acc = acc + jnp.dot(a, wd_ref[cols, :], preferred_element_type=F32)
    out = h + (0.5 * (1.0 + mod_ref[2:3, :])) * acc
    if final:
        fmod_ref, fgain_ref = rest[0], rest[1]
        out = _norm_mod(out, fgain_ref[...], fmod_ref[0:1, :], fmod_ref[1:2, :])
    o_ref[...] = out


def _ffn(h, mod3, gain, layer, wg, wu, wd, final_mod=None, final_gain=None):
    B, S, D = h.shape
    tm = min(FFN_TILE, S)
    const = lambda b, i: (0, 0)
    of_layer = lambda b, i: (layer, 0, 0)
    final = final_mod is not None
    extra_specs = [pl.BlockSpec((None, 2, D), lambda b, i: (b, 0, 0)), pl.BlockSpec((1, D), const)] if final else []
    extra = (final_mod, final_gain) if final else ()
    return pl.pallas_call(
        functools.partial(_ffn_kernel, final=final),
        grid=(B, S // tm),
        in_specs=[pl.BlockSpec((None, tm, D), lambda b, i: (b, i, 0)),
                  pl.BlockSpec((None, 3, D), lambda b, i: (b, 0, 0)),
                  pl.BlockSpec((1, D), const),
                  pl.BlockSpec((None, D, D_FF), of_layer),
                  pl.BlockSpec((None, D, D_FF), of_layer),
                  pl.BlockSpec((None, D_FF, D), of_layer)] + extra_specs,
        out_specs=pl.BlockSpec((None, tm, D), lambda b, i: (b, i, 0)),
        out_shape=jax.ShapeDtypeStruct((B, S, D), F32),
        compiler_params=_cparams("arbitrary", "arbitrary"),
        name="ffn",
    )(h, mod3, gain, wg, wu, wd, *extra)


def _inproj_kernel(h_ref, mod_ref, gain_ref, w_ref, wdt_ref, cos_ref, sin_ref, convw_ref, convb_ref,
                   proj_ref, dt_ref, xbuf_ref):
    tm = h_ref.shape[0]

    @pl.when(pl.program_id(1) == 0)
    def _():
        xbuf_ref[0:CONV_PAD, :] = jnp.zeros((CONV_PAD, SSM_XBC), F32)

    u = _norm_mod(h_ref[...], gain_ref[...], mod_ref[0:1, :], mod_ref[1:2, :]).astype(BF16)
    cos2 = cos_ref[...]
    sin2 = sin_ref[...]

    def rope(x):
        heads = [x[:, a * RET_DIM:(a + 1) * RET_DIM] for a in range(x.shape[1] // RET_DIM)]
        return jnp.concatenate([y * cos2 + pltpu.roll(y, RET_DIM // 2, 1) * sin2 for y in heads], axis=1)

    post = {0: rope,
            1: lambda y: rope(y) * (RET_DIM ** -0.5),
            3: _silu,
            4: lambda y: y * (SB_DIM ** -0.5),
            7: _silu}
    def project(j):
        return jnp.dot(u, w_ref[:, j * GROUP_W:(j + 1) * GROUP_W], preferred_element_type=F32)

    xbc0 = (PROJ_W - SSM_XBC) // GROUP_W
    for j in range(SSM_XBC // GROUP_W):
        xbuf_ref[CONV_PAD:CONV_PAD + tm, j * GROUP_W:(j + 1) * GROUP_W] = project(xbc0 + j)
    dt_ref[...] = jnp.dot(u, wdt_ref[...], preferred_element_type=F32)
    for j in range(SSM_XBC // GROUP_W):
        cols = slice(j * GROUP_W, (j + 1) * GROUP_W)
        rows = xbuf_ref[:, cols]
        conv = convb_ref[:, cols] + convw_ref[SSM_CONV - 1:SSM_CONV, cols] * rows[CONV_PAD:]
        for back in range(1, SSM_CONV):
            tap = convw_ref[SSM_CONV - 1 - back:SSM_CONV - back, cols]
            conv = conv + tap * pltpu.roll(rows, back, 0)[CONV_PAD:]
        proj_ref[:, (xbc0 + j) * GROUP_W:(xbc0 + j + 1) * GROUP_W] = _silu(conv).astype(proj_ref.dtype)
    xbuf_ref[0:CONV_PAD, :] = xbuf_ref[tm:tm + CONV_PAD, :]
    for j in range(xbc0):
        cols = slice(j * GROUP_W, (j + 1) * GROUP_W)
        proj_ref[:, cols] = post.get(j, lambda y: y)(project(j)).astype(proj_ref.dtype)


def _inproj(h, mod3, gain, layer, w_main, w_dt, cos2, sin2, conv_w, conv_b):
    B, S, D = h.shape
    tm = min(TOKEN_TILE, S)
    const = lambda b, i: (0, 0)
    of_layer = lambda b, i: (layer, 0, 0)
    return pl.pallas_call(
        _inproj_kernel,
        grid=(B, S // tm),
        in_specs=[pl.BlockSpec((None, tm, D), lambda b, i: (b, i, 0)),
                  pl.BlockSpec((None, 3, D), lambda b, i: (b, 0, 0)),
                  pl.BlockSpec((1, D), const),
                  pl.BlockSpec((None, D, PROJ_W), of_layer),
                  pl.BlockSpec((None, D, DT_PAD), of_layer),
                  pl.BlockSpec((tm, RET_DIM), lambda b, i: (i, 0)),
                  pl.BlockSpec((tm, RET_DIM), lambda b, i: (i, 0)),
                  pl.BlockSpec((SSM_CONV, SSM_XBC), const),
                  pl.BlockSpec((1, SSM_XBC), const)],
        out_specs=[pl.BlockSpec((None, tm, PROJ_W), lambda b, i: (b, i, 0)),
                   pl.BlockSpec((None, tm, DT_PAD), lambda b, i: (b, i, 0))],
        out_shape=[jax.ShapeDtypeStruct((B, S, PROJ_W), BF16),
                   jax.ShapeDtypeStruct((B, S, DT_PAD), F32)],
        scratch_shapes=[pltpu.VMEM((tm + CONV_PAD, SSM_XBC), F32)],
        compiler_params=_cparams("arbitrary", "arbitrary"),
        name="inproj",
    )(h, mod3, gain, w_main, w_dt, cos2, sin2, conv_w, conv_b.reshape(1, SSM_XBC))


def _rope_tables(S):
    half = RET_DIM // 2
    pos = jnp.arange(S, dtype=F32)
    inv_freq = ROPE_BASE ** (-jnp.arange(half, dtype=F32) / half)
    ang = pos[:, None] * inv_freq[None, :]
    cos, sin = jnp.cos(ang), jnp.sin(ang)
    return jnp.concatenate([cos, cos], axis=-1), jnp.concatenate([-sin, sin], axis=-1)


def _run(stages):
    return _interleave(stages)[0]


def _interleave(*staged):
    values = [None] * len(staged)
    live = list(enumerate(staged))
    while live:
        still = []
        for n, stages in live:
            try:
                next(stages)
                still.append((n, stages))
            except StopIteration as done:
                values[n] = done.value
        live = still
    return values


def _delayed(stages, rounds):
    for _ in range(rounds):
        yield
    return (yield from stages)


def _ret_tile(q_ref, k_ref, v_ref, g_ref, dmat_ref, qdec_ref, kdec_ref, sdec_ref, gn_ref, state_ref, wout_ref):
    heads = range(RET_HEADS)
    cols = [slice(h * RET_DIM, (h + 1) * RET_DIM) for h in heads]
    nt = (((1,), (1,)), ((), ()))
    tn = (((0,), (0,)), ((), ()))
    state = [state_ref[h] for h in heads]
    scores = [lax.dot_general(q_ref[:, c], k_ref[:, c], nt, preferred_element_type=F32) for c in cols]
    cross = [jnp.dot(q_ref[:, cols[h]], state[h].astype(BF16), preferred_element_type=F32) for h in heads]
    yield
    vdec = [(kdec_ref[h] * v_ref[:, cols[h]].astype(F32)).astype(BF16) for h in heads]
    kv = [lax.dot_general(k_ref[:, cols[h]], vdec[h], tn, preferred_element_type=F32) for h in heads]
    yield
    weights = [(scores[h] * dmat_ref[h]).astype(BF16) for h in heads]
    yield
    y = [jnp.dot(weights[h], v_ref[:, cols[h]], preferred_element_type=F32) + qdec_ref[h] * cross[h] for h in heads]
    yield
    outs = []
    for h in heads:
        state_ref[h] = state[h] * sdec_ref[h, 0:1, :] + kv[h]
        yn = y[h] * lax.rsqrt(jnp.mean(y[h] * y[h], axis=-1, keepdims=True) + NORM_EPS) * gn_ref[h]
        outs.append((yn * g_ref[:, cols[h]].astype(F32)).astype(BF16))
    yield
    return sum(jnp.dot(outs[h], wout_ref[cols[h], :], preferred_element_type=F32) for h in heads)


def _retention_consts(S):
    T = min(MIX_TILE, S)
    H, Dh = RET_HEADS, RET_DIM
    log_gamma = jnp.log1p(-(2.0 ** (-5.0 - jnp.arange(H, dtype=F32))))
    idx = jnp.arange(T, dtype=F32)
    chunk_of = jnp.arange(T) // CHUNK
    seen = chunk_of[None, :] <= chunk_of[:, None]
    dmat = jnp.where(seen[None], jnp.exp(log_gamma[:, None, None] * jnp.abs(idx[:, None] - idx[None, :])), 0.0)
    qdec = jnp.exp(log_gamma[:, None] * (idx + 1.0)[None, :])
    kdec = jnp.exp(log_gamma[:, None] * (T - 1 - idx)[None, :])
    sdec = jnp.exp(log_gamma * T)
    bc = lambda a: jnp.broadcast_to(a[:, :, None], (H, T, Dh))
    return dmat, bc(qdec), bc(kdec), jnp.broadcast_to(sdec[:, None, None], (H, 8, Dh))


def _split2(x):
    hi = x.astype(BF16)
    return hi, (x - hi.astype(F32)).astype(BF16)


def _split3(x):
    hi = x.astype(BF16)
    rest = x - hi.astype(F32)
    mid = rest.astype(BF16)
    return hi, mid, (rest - mid.astype(F32)).astype(BF16)


RET_DELAY = 4
SB_STAGES = 5
SB_CARRY_STAGE = 4


def _sb_blocks(q_ref, k_ref, v_ref, acc_ref, carry_ref, j, diagonal, tq, exists=None):
    def visible(r, n):
        shape = (r.stop - r.start, n)
        return lax.broadcasted_iota(jnp.int32, shape, 1) < lax.broadcasted_iota(jnp.int32, shape, 0) + r.start

    def later(n):
        return jnp.where(lax.broadcasted_iota(jnp.int32, (n, n), 0) > lax.broadcasted_iota(jnp.int32, (n, n), 1),
                         1.0, 0.0).astype(BF16)

    start = pl.multiple_of(j * tq, tq)
    half = tq // 2
    row_panels = [(slice(0, half), half), (slice(half, tq), tq)] if diagonal else [(slice(0, tq), tq)]
    panels = [(r, n, slice(h * SB_DIM, (h + 1) * SB_DIM)) for h in range(SB_HEADS) for r, n in row_panels]
    heads_of = [h for h in range(SB_HEADS) for _ in row_panels]

    z = [lax.dot_general(q_ref[r, c], k_ref[pl.ds(start, n), c], (((1,), (1,)), ((), ())),
                         preferred_element_type=F32) for r, n, c in panels]
    yield
    log_beta = [jnp.minimum(x, 0.0) - jnp.log(1.0 + jnp.exp(-jnp.abs(x))) for x in z]
    log_keep = [lb - x for lb, x in zip(log_beta, z)]
    if diagonal:
        log_keep = [jnp.where(visible(r, n), lk, 0.0) for lk, (r, n, _) in zip(log_keep, panels)]
    if exists is not None:
        log_keep = [jnp.where(exists, lk, 0.0) for lk in log_keep]
    pieces = [_split2(lk) for lk in log_keep]
    yield
    tail_in = [jnp.dot(hi, later(n), preferred_element_type=F32) + jnp.dot(lo, later(n), preferred_element_type=F32)
               for (hi, lo), (_, n, _) in zip(pieces, panels)]
    yield
    total = [t[:, 0:1] + lk[:, 0:1] for t, lk in zip(tail_in, log_keep)]
    if diagonal:
        w = [jnp.where(visible(r, n), jnp.exp(lb + t), 0.0) for lb, t, (r, n, _) in zip(log_beta, tail_in, panels)]
    else:
        carry = [carry_ref[h, r, :] for h, (r, _, _) in zip(heads_of, panels)]
        w = [jnp.exp(lb + t + cy) for lb, t, cy in zip(log_beta, tail_in, carry)]
        total = [cy + tt for cy, tt in zip(carry, total)]
    if exists is not None:
        w = [jnp.where(exists, x, 0.0) for x in w]
    yield
    out = [jnp.dot(x.astype(BF16), v_ref[pl.ds(start, n), c], preferred_element_type=F32)
           for x, (_, n, c) in zip(w, panels)]
    for h, (r, _, c), tot, o in zip(heads_of, panels, total, out):
        carry_ref[h, r, :] = tot
        if diagonal:
            acc_ref[r, c] = o
        else:
            acc_ref[r, c] += o


def _sb_earlier_blocks(q_ref, k_ref, v_ref, acc_ref, carry_ref, i, tq):
    refs = (q_ref, k_ref, v_ref, acc_ref, carry_ref)

    def cond(s):
        j, top = s
        return jnp.logical_and(j >= 0, top > SB_DEAD_LOG_WEIGHT)

    def body(s):
        j, _ = s
        _run(_sb_blocks(*refs, j, False, tq))
        return j - 1, jnp.max(carry_ref[...])

    j_end, _ = lax.while_loop(cond, body, (i - 2, jnp.max(carry_ref[...])))
    return i - 2 - j_end


def _expand_heads(v):
    T = v.shape[0]
    lane = lax.broadcasted_iota(jnp.int32, (T, LANES), 1)
    tiles = []
    for j in range(SSM_W // LANES):
        a = jnp.broadcast_to(v[:, 2 * j:2 * j + 1], (T, LANES))
        b = jnp.broadcast_to(v[:, 2 * j + 1:2 * j + 2], (T, LANES))
        tiles.append(jnp.where(lane < SSM_HEAD_DIM, a, b))
    return jnp.concatenate(tiles, axis=1)


def _ssm_tile(xc_ref, z_ref, dt_ref, dtb_ref, alog_ref, dskip_ref, gain_ref, state_ref, wout_ref, T):
    G, Hg, P, N = SSM_GROUPS, SSM_HEADS // SSM_GROUPS, SSM_HEAD_DIM, SSM_STATE
    xs = xc_ref[:, :SSM_W].astype(F32)
    bm = [xc_ref[:, SSM_W + g * N:SSM_W + (g + 1) * N] for g in range(G)]
    cm = [xc_ref[:, SSM_W + (G + g) * N:SSM_W + (G + g + 1) * N] for g in range(G)]

    dt_in = dt_ref[...] + dtb_ref[...]
    dt = jnp.maximum(dt_in, 0.0) + jnp.log1p(jnp.exp(-jnp.abs(dt_in)))
    da = dt * (-jnp.exp(alog_ref[...]))
    row = lax.broadcasted_iota(jnp.int32, (T, T), 0)
    col = lax.broadcasted_iota(jnp.int32, (T, T), 1)
    causal = row >= col
    ones_below = jnp.where(causal, 1.0, 0.0).astype(BF16)
    acum = sum(jnp.dot(ones_below, piece, preferred_element_type=F32) for piece in _split3(da))
    group_cols = [slice(g * Hg * P, (g + 1) * Hg * P) for g in range(G)]
    half = T // 2
    top, bot = slice(0, half), slice(half, T)
    causal_top = (lax.broadcasted_iota(jnp.int32, (half, half), 0)
                  >= lax.broadcasted_iota(jnp.int32, (half, half), 1))
    causal_bot = (lax.broadcasted_iota(jnp.int32, (T - half, T), 0) + half
                  >= lax.broadcasted_iota(jnp.int32, (T - half, T), 1))
    nt = (((1,), (1,)), ((), ()))
    cb_top = [lax.dot_general(cm[g][top], bm[g][top], nt, preferred_element_type=F32) for g in range(G)]
    cb_bot = [lax.dot_general(cm[g][bot], bm[g], nt, preferred_element_type=F32) for g in range(G)]
    state = [state_ref[g] for g in range(G)]
    yield
    acum_t = acum.T
    last = acum[T - 1:T, :]
    dt_x = _expand_heads(dt)
    grow_x = _expand_heads(jnp.exp(acum))
    end_x = _expand_heads(jnp.exp(last - acum))
    xdt = xs * dt_x
    xdt_b = xdt.astype(BF16)
    xend_b = (xdt * end_x).astype(BF16)
    yield
    y_inter = [jnp.dot(cm[g], state[g].astype(BF16), preferred_element_type=F32) * grow_x[:, group_cols[g]]
               for g in range(G)]
    w_top, w_bot = [], []
    for h in range(SSM_HEADS):
        decay_top = jnp.where(causal_top, jnp.exp(acum[top, h:h + 1] - acum_t[h:h + 1, top]), 0.0)
        decay_bot = jnp.where(causal_bot, jnp.exp(acum[bot, h:h + 1] - acum_t[h:h + 1, :]), 0.0)
        w_top.append((cb_top[h // Hg] * decay_top).astype(BF16))
        w_bot.append((cb_bot[h // Hg] * decay_bot).astype(BF16))
        if h % 2 == 1:
            yield
    y_intra = [jnp.concatenate([
        jnp.dot(w_top[h], xdt_b[top, h * P:(h + 1) * P], preferred_element_type=F32),
        jnp.dot(w_bot[h], xdt_b[:, h * P:(h + 1) * P], preferred_element_type=F32)], axis=0)
        for h in range(SSM_HEADS)]
    for g in range(G):
        state_ref[g] = state[g] * grow_x[T - 1:T, group_cols[g]] + lax.dot_general(
            bm[g], xend_b[:, group_cols[g]], (((0,), (0,)), ((), ())), preferred_element_type=F32)
    yield
    y = jnp.concatenate(y_intra, axis=1) + jnp.concatenate(y_inter, axis=1) + xs * dskip_ref[...]
    y = y * z_ref[...].astype(F32)
    y = (y * lax.rsqrt(jnp.mean(y * y, axis=-1, keepdims=True) + NORM_EPS) * gain_ref[...]).astype(BF16)
    yield
    return jnp.dot(y, wout_ref[...], preferred_element_type=F32)


def _mixer_kernel(proj_ref, sk_ref, sv_ref, dt_ref,
                  h_ref, mod_ref, wout_ref, dmat_ref, qdec_ref, kdec_ref, sdec_ref, gn_ref,
                  dtb_ref, alog_ref, dskip_ref, sgain_ref,
                  o_ref, ret_state_ref, ssm_state_ref, sb_acc_ref, sb_carry_ref, mixed_ref, *, T):
    i = pl.program_id(1)
    group = lambda n: proj_ref.at[:, n * GROUP_W:(n + 1) * GROUP_W]
    rq_ref, rk_ref, rv_ref, rg_ref, sq_ref, mz_ref = group(0), group(1), group(2), group(3), group(4), group(7)
    mxc_ref = proj_ref.at[:, PROJ_W - SSM_XBC:PROJ_W]

    @pl.when(i == 0)
    def _():
        ret_state_ref[...] = jnp.zeros_like(ret_state_ref)
        ssm_state_ref[...] = jnp.zeros_like(ssm_state_ref)

    sb_refs = (sq_ref, sk_ref, sv_ref, sb_acc_ref, sb_carry_ref)
    _, mixed_ssm, mixed_ret, _ = _interleave(
        _sb_blocks(*sb_refs, i, True, T),
        _ssm_tile(mxc_ref, mz_ref, dt_ref, dtb_ref, alog_ref, dskip_ref, sgain_ref, ssm_state_ref,
                  wout_ref.at[RET_W + SB_W:MIX_W], T),
        _delayed(_ret_tile(rq_ref, rk_ref, rv_ref, rg_ref, dmat_ref, qdec_ref, kdec_ref, sdec_ref, gn_ref,
                           ret_state_ref, wout_ref.at[0:RET_W]), RET_DELAY),
        _delayed(_sb_blocks(*sb_refs, jnp.maximum(i - 1, 0), False, T, exists=i > 0), SB_STAGES - SB_CARRY_STAGE + 1))
    mixed_ref[...] = mixed_ssm + mixed_ret

    def finish():
        mixed = mixed_ref[...] + jnp.dot(sb_acc_ref[...].astype(BF16), wout_ref[RET_W:RET_W + SB_W, :],
                                         preferred_element_type=F32)
        o_ref[...] = h_ref[...] + (1.0 + mod_ref[2:3, :]) * mixed

    finish()
    walked = _sb_earlier_blocks(*sb_refs, i, T)
    pl.when(walked > 0)(finish)


def _mixer(proj, dt_pad, h, mod3, layer, w_out, ret_consts, gn, dt_bias, a_log, d_skip, sgain):
    B, S, D = h.shape
    T = min(MIX_TILE, S)
    dmat, qdec, kdec, sdec = ret_consts
    whole_seq = lambda n: pl.BlockSpec((None, S, GROUP_W), lambda b, i: (b, 0, n))
    whole = lambda a: pl.BlockSpec(a.shape, lambda b, i: (0,) * a.ndim)
    pad_heads = lambda a: jnp.pad(a.reshape(1, SSM_HEADS), ((0, 0), (0, DT_PAD - SSM_HEADS)))
    small = (dmat, qdec, kdec, sdec, gn, pad_heads(dt_bias), pad_heads(a_log),
             jnp.repeat(d_skip, SSM_HEAD_DIM).reshape(1, SSM_W), sgain.reshape(1, SSM_W))
    return pl.pallas_call(
        functools.partial(_mixer_kernel, T=T),
        grid=(B, S // T),
        in_specs=[pl.BlockSpec((None, T, PROJ_W), lambda b, i: (b, i, 0)),
                  whole_seq(5), whole_seq(6),
                  pl.BlockSpec((None, T, DT_PAD), lambda b, i: (b, i, 0)),
                  pl.BlockSpec((None, T, D), lambda b, i: (b, i, 0)),
                  pl.BlockSpec((None, 3, D), lambda b, i: (b, 0, 0)),
                  pl.BlockSpec((None, MIX_W, D), lambda b, i: (layer, 0, 0))] + [whole(a) for a in small],
        out_specs=pl.BlockSpec((None, T, D), lambda b, i: (b, i, 0)),
        out_shape=jax.ShapeDtypeStruct((B, S, D), F32),
        scratch_shapes=[pltpu.VMEM((RET_HEADS, RET_DIM, RET_DIM), F32),
                        pltpu.VMEM((SSM_GROUPS, SSM_STATE, SSM_W // SSM_GROUPS), F32),
                        pltpu.VMEM((T, SB_W), F32),
                        pltpu.VMEM((SB_HEADS, T, 1), F32),
                        pltpu.VMEM((T, D), F32)],
        compiler_params=_cparams("arbitrary", "arbitrary", vmem_limit=MIXER_VMEM_LIMIT),
        name="mixer",
    )(proj, proj, proj, dt_pad, h, mod3, w_out, *small)


def kernel(x, c, ada_w, ada_b, norm_ffn1, ffn1_wg, ffn1_wu, ffn1_wd, norm_mix, w_in, conv_w, conv_b,
           dt_bias, a_log, d_skip, ret_gn, ssm_norm, w_out, norm_ffn2, ffn2_wg, ffn2_wu, ffn2_wd,
           final_ada_w, final_ada_b, final_norm):
    B, S, D = x.shape
    assert D == D_MODEL and S % CHUNK == 0 and B <= 8
    c_pad = jnp.pad(c, ((0, 8 - B), (0, 0)))
    mod = _ada_proj(c_pad, ada_w, ada_b[:, None, :], 3 * N_SUB * D // 4)[:, :B].reshape(DEPTH, B, 3 * N_SUB, D)
    fmod = _ada_proj(c_pad, final_ada_w[None], final_ada_b[None, None, :], 1024)[0, :B].reshape(B, 2, D)
    cos2, sin2 = _rope_tables(S)
    ret_consts = _retention_consts(S)
    ffn1_w = [_to_bf16(w) for w in (ffn1_wg, ffn1_wu, ffn1_wd)]
    ffn2_w = [_to_bf16(w) for w in (ffn2_wg, ffn2_wu, ffn2_wd)]
    w_main = w_in.astype(BF16)
    w_dt = jnp.pad(w_main[:, :, PROJ_W:], ((0, 0), (0, 0), (0, DT_PAD - SSM_HEADS)))
    w_out_b = _to_bf16(w_out)

    h = x
    for l in range(DEPTH):
        h = _ffn(h, mod[l, :, 0:3], norm_ffn1[l][None], l, *ffn1_w)
        proj, dt_pad = _inproj(h, mod[l, :, 3:6], norm_mix[l][None], l, w_main, w_dt, cos2, sin2,
                               conv_w[l], conv_b[l])
        h = _mixer(proj, dt_pad, h, mod[l, :, 3:6], l, w_out_b, ret_consts,
                   ret_gn[l].reshape(RET_HEADS, 1, RET_DIM), dt_bias[l], a_log[l], d_skip[l], ssm_norm[l])
        last = l == DEPTH - 1
        h = _ffn(h, mod[l, :, 6:9], norm_ffn2[l][None], l, *ffn2_w,
                 final_mod=fmod if last else None, final_gain=final_norm[None] if last else None)
    return h
```
